```python
import math
import jax, jax.numpy as jnp
from jax import lax
import numpy as np

D_MODEL = 1024
BATCH = 2
SEQ = 8192
DEPTH = 1
DEC_BATCH = 32
DEC_SEQ = 64
PAST_LEN = 4096

CHUNK = 64
EPS = 1e-6
POOL_WINDOWS = (2, 4, 8, 16)
N_POOL_GROUPS = 4
POOL_WIDTH = D_MODEL
POOL_GROUP = POOL_WIDTH // N_POOL_GROUPS
POOL_STATE = max(POOL_WINDOWS) - 1
SSM_EXPAND = 2
D_INNER = SSM_EXPAND * D_MODEL
SSM_HEAD_DIM = 64
N_SSM_HEADS = D_INNER // SSM_HEAD_DIM
N_SSM_GROUPS = 4
HEADS_PER_GROUP = N_SSM_HEADS // N_SSM_GROUPS
D_STATE = 128
CONV_WIDTH = 4
CONV_DIM = D_INNER + 2 * N_SSM_GROUPS * D_STATE
SSD_BLOCK = CHUNK
D_FF = ((8 * D_MODEL // 3 + 255) // 256) * 256
P_END = POOL_WIDTH
Z_END = P_END + D_INNER
XBC_END = Z_END + CONV_DIM
DT_END = XBC_END + N_SSM_HEADS
GA_END = DT_END + D_MODEL
N_IN = GA_END + D_MODEL

kernel_name = "pool_ssd_gated_hybrid_stream_step"


def _rmsnorm(x, w):
    xf = x.astype(jnp.float32)
    y = xf * lax.rsqrt(jnp.mean(xf * xf, axis=-1, keepdims=True) + EPS)
    return (y * w.astype(jnp.float32)).astype(x.dtype)


def _pool_mixer(p, p_hist, pos0, w_group, pool_scale):
    b, L, _ = p.shape
    full = jnp.concatenate([p_hist.astype(p.dtype), p], axis=1)
    pf = full.astype(jnp.float32)
    csum = jnp.concatenate([jnp.zeros((b, 1, POOL_WIDTH), jnp.float32), jnp.cumsum(pf, axis=1)], axis=1)
    end = csum[:, POOL_STATE + 1:]
    pos = pos0 + jnp.arange(L)
    outs = []
    for g, w in enumerate(POOL_WINDOWS):
        sl = slice(g * POOL_GROUP, (g + 1) * POOL_GROUP)
        start = csum[:, POOL_STATE + 1 - w: POOL_STATE + 1 - w + L, sl]
        cnt = jnp.minimum(pos + 1, w).astype(jnp.float32)[None, :, None]
        outs.append((end[..., sl] - start) / cnt - pf[:, POOL_STATE:, sl])
    m = jnp.stack(outs, axis=2)
    y = jnp.einsum('blgc,gcd->blgd', m, w_group.astype(jnp.float32)).reshape(b, L, POOL_WIDTH)
    y = y * pool_scale.astype(jnp.float32)
    return y.astype(p.dtype), full[:, -POOL_STATE:]


def _causal_conv(u, u_hist, w, bias):
    full = jnp.concatenate([u_hist.astype(u.dtype), u], axis=1)
    y = lax.conv_general_dilated(full, w[:, None, :].astype(u.dtype), (1,), 'VALID',
                                 dimension_numbers=('NWC', 'WIO', 'NWC'),
                                 feature_group_count=CONV_DIM)
    return y + bias, full[:, -(CONV_WIDTH - 1):]


def _ssd_scan(x, dt, A, Bm, Cm, h0, block):
    b, L = x.shape[:2]
    nc = L // block
    G, R, P, N = N_SSM_GROUPS, HEADS_PER_GROUP, SSM_HEAD_DIM, D_STATE
    f32 = jnp.float32
    x = x.reshape(b, nc, block, G, R, P).astype(f32)
    dt = dt.reshape(b, nc, block, G, R)
    Bm = Bm.reshape(b, nc, block, G, N).astype(f32)
    Cm = Cm.reshape(b, nc, block, G, N).astype(f32)
    acs = jnp.cumsum(dt * A.reshape(G, R), axis=2)
    mask = jnp.tril(jnp.ones((block, block), bool))[:, :, None, None]
    seg = acs[:, :, :, None] - acs[:, :, None, :]
    decay = jnp.exp(jnp.where(mask, seg, -jnp.inf))
    cb = jnp.einsum('bclgn,bcsgn->bclsg', Cm, Bm)
    y_diag = jnp.einsum('bclsg,bclsgr,bcsgr,bcsgrp->bclgrp', cb, decay, dt, x)
    decay_to_end = jnp.exp(acs[:, :, -1:] - acs)
    states = jnp.einsum('bclgn,bclgr,bclgrp->bcgrpn', Bm, decay_to_end * dt, x)
    block_decay = jnp.exp(acs[:, :, -1])

    def step(h, inp):
        s, d = inp
        return d[..., None, None] * h + s, h

    h_init = h0.reshape(b, G, R, P, N).astype(f32)
    h_last, h_prev = lax.scan(step, h_init, (jnp.moveaxis(states, 1, 0), jnp.moveaxis(block_decay, 1, 0)))
    h_prev = jnp.moveaxis(h_prev, 0, 1)
    y_off = jnp.einsum('bclgn,bcgrpn,bclgr->bclgrp', Cm, h_prev, jnp.exp(acs))
    y = (y_diag + y_off).reshape(b, L, N_SSM_HEADS, P)
    return y, h_last.reshape(b, N_SSM_HEADS, P, N)


def _ssd_mixer(z, xbc, dt_raw, conv_hist, h0, block, conv_w, conv_b, dt_bias, a_log, d_skip, norm_w):
    b, L, _ = z.shape
    xbc_c, new_conv = _causal_conv(xbc, conv_hist, conv_w, conv_b)
    xbc_c = jax.nn.silu(xbc_c)
    xs = xbc_c[..., :D_INNER].reshape(b, L, N_SSM_HEADS, SSM_HEAD_DIM)
    Bm = xbc_c[..., D_INNER:D_INNER + N_SSM_GROUPS * D_STATE].reshape(b, L, N_SSM_GROUPS, D_STATE)
    Cm = xbc_c[..., D_INNER + N_SSM_GROUPS * D_STATE:].reshape(b, L, N_SSM_GROUPS, D_STATE)
    dt = jax.nn.softplus(dt_raw.astype(jnp.float32) + dt_bias.astype(jnp.float32))
    A = -jnp.exp(a_log.astype(jnp.float32))
    y, h_last = _ssd_scan(xs, dt, A, Bm, Cm, h0, block)
    y = y + d_skip.astype(jnp.float32)[:, None] * xs.astype(jnp.float32)
    y = y.reshape(b, L, D_INNER) * jax.nn.silu(z.astype(jnp.float32))
    yg = y.reshape(b, L, N_SSM_GROUPS, D_INNER // N_SSM_GROUPS)
    yg = yg * lax.rsqrt(jnp.mean(yg * yg, axis=-1, keepdims=True) + EPS)
    y = yg.reshape(b, L, D_INNER) * norm_w.astype(jnp.float32)
    return y.astype(z.dtype), new_conv, h_last.astype(h0.dtype)


def _layer(x, c, pool_hist, conv_hist, ssm_h, pos0, block,
           w_ada, b_ada, norm1_w, norm2_w, w_in, pool_group_w, pool_scale, conv_w, conv_b,
           dt_bias, a_log, d_skip, ssm_norm_w, w_branch_a, w_branch_b, w_out,
           w_ffn_gate, w_ffn_up, w_ffn_down):
    mod = jax.nn.silu(c) @ w_ada + b_ada
    shift1, scale1, gate1, shift2, scale2, gate2 = jnp.split(mod[:, None, :], 6, axis=-1)
    h = _rmsnorm(x, norm1_w) * (1 + scale1) + shift1
    u = h @ w_in
    p = u[..., :P_END]
    z = u[..., P_END:Z_END]
    xbc = u[..., Z_END:XBC_END]
    dt_raw = u[..., XBC_END:DT_END]
    g_a = u[..., DT_END:GA_END]
    g_b = u[..., GA_END:]
    a_out, new_pool = _pool_mixer(p, pool_hist, pos0, pool_group_w, pool_scale)
    b_out, new_conv, new_ssm = _ssd_mixer(z, xbc, dt_raw, conv_hist, ssm_h, block, conv_w, conv_b,
                                          dt_bias, a_log, d_skip, ssm_norm_w)
    merged = jax.nn.sigmoid(g_a) * (a_out @ w_branch_a) + jax.nn.sigmoid(g_b) * (b_out @ w_branch_b)
    x = x + gate1 * (merged @ w_out)
    h2 = _rmsnorm(x, norm2_w) * (1 + scale2) + shift2
    f = (jax.nn.silu(h2 @ w_ffn_gate) * (h2 @ w_ffn_up)) @ w_ffn_down
    x = x + gate2 * f
    return x, new_pool, new_conv, new_ssm


def setup_inputs(seed: int = 0) -> dict:
    key = jax.random.key(seed)
    ks = jax.random.split(key, 32)
    f32 = jnp.float32

    def nrm(k, shape, scale):
        return jax.random.normal(k, shape, f32) * scale

    L = DEPTH
    dt_init = jnp.exp(jax.random.uniform(ks[16], (L, N_SSM_HEADS), f32, math.log(1e-3), math.log(1e-1)))
    return {
        "x_prompt": nrm(ks[0], (BATCH, SEQ, D_MODEL), 1.0),
        "x_sample": nrm(ks[1], (DEC_BATCH, DEC_SEQ, D_MODEL), 1.0),
        "c_prompt": nrm(ks[2], (BATCH, D_MODEL), 1.0),
        "c_sample": nrm(ks[3], (DEC_BATCH, D_MODEL), 1.0),
        "state_pool": nrm(ks[4], (L, DEC_BATCH, POOL_STATE, POOL_WIDTH), 1.0),
        "state_conv": nrm(ks[5], (L, DEC_BATCH, CONV_WIDTH - 1, CONV_DIM), 1.0),
        "state_ssm": nrm(ks[6], (L, DEC_BATCH, N_SSM_HEADS, SSM_HEAD_DIM, D_STATE), 0.1),
        "w_ada": nrm(ks[7], (L, D_MODEL, 6 * D_MODEL), 0.5 * D_MODEL ** -0.5),
        "b_ada": nrm(ks[8], (L, 6 * D_MODEL), 0.02),
        "norm1_w": 1.0 + nrm(ks[9], (L, D_MODEL), 0.02),
        "norm2_w": 1.0 + nrm(ks[10], (L, D_MODEL), 0.02),
        "w_in": nrm(ks[11], (L, D_MODEL, N_IN), D_MODEL ** -0.5),
        "pool_group_w": nrm(ks[12], (L, N_POOL_GROUPS, POOL_GROUP, POOL_GROUP), POOL_GROUP ** -0.5),
        "pool_scale": 1.0 + nrm(ks[13], (L, POOL_WIDTH), 0.1),
        "conv_w": nrm(ks[14], (L, CONV_WIDTH, CONV_DIM), CONV_WIDTH ** -0.5),
        "conv_b": nrm(ks[15], (L, CONV_DIM), 0.02),
        "dt_bias": dt_init + jnp.log(-jnp.expm1(-dt_init)),
        "a_log": jnp.log(jax.random.uniform(ks[17], (L, N_SSM_HEADS), f32, 1.0, 16.0)),
        "d_skip": 1.0 + nrm(ks[18], (L, N_SSM_HEADS), 0.1),
        "ssm_norm_w": 1.0 + nrm(ks[19], (L, D_INNER), 0.02),
        "w_branch_a": nrm(ks[20], (L, POOL_WIDTH, D_MODEL), POOL_WIDTH ** -0.5),
        "w_branch_b": nrm(ks[21], (L, D_INNER, D_MODEL), D_INNER ** -0.5),
        "w_out": nrm(ks[22], (L, D_MODEL, D_MODEL), D_MODEL ** -0.5),
        "w_ffn_gate": nrm(ks[23], (L, D_MODEL, D_FF), D_MODEL ** -0.5),
        "w_ffn_up": nrm(ks[24], (L, D_MODEL, D_FF), D_MODEL ** -0.5),
        "w_ffn_down": nrm(ks[25], (L, D_FF, D_MODEL), D_FF ** -0.5),
        "final_norm_w": 1.0 + nrm(ks[26], (D_MODEL,), 0.02),
    }


def reference(x_prompt, x_sample, c_prompt, c_sample, state_pool, state_conv, state_ssm,
              w_ada, b_ada, norm1_w, norm2_w, w_in, pool_group_w, pool_scale, conv_w, conv_b,
              dt_bias, a_log, d_skip, ssm_norm_w, w_branch_a, w_branch_b, w_out,
              w_ffn_gate, w_ffn_up, w_ffn_down, final_norm_w):
    xp, xs = x_prompt, x_sample
    bp = xp.shape[0]
    pool_p, conv_p, ssm_p, pool_s, conv_s, ssm_s = [], [], [], [], [], []
    for l in range(DEPTH):
        params = (w_ada[l], b_ada[l], norm1_w[l], norm2_w[l], w_in[l], pool_group_w[l], pool_scale[l],
                  conv_w[l], conv_b[l], dt_bias[l], a_log[l], d_skip[l], ssm_norm_w[l],
                  w_branch_a[l], w_branch_b[l], w_out[l], w_ffn_gate[l], w_ffn_up[l], w_ffn_down[l])
        xp, sp_, cp_, hp_ = _layer(
            xp, c_prompt,
            jnp.zeros((bp, POOL_STATE, POOL_WIDTH), xp.dtype),
            jnp.zeros((bp, CONV_WIDTH - 1, CONV_DIM), xp.dtype),
            jnp.zeros((bp, N_SSM_HEADS, SSM_HEAD_DIM, D_STATE), state_ssm.dtype),
            0, SSD_BLOCK, *params)
        xs, ss_, cs_, hs_ = _layer(
            xs, c_sample, state_pool[l], state_conv[l], state_ssm[l],
            PAST_LEN, xs.shape[1], *params)
        pool_p.append(sp_); conv_p.append(cp_); ssm_p.append(hp_)
        pool_s.append(ss_); conv_s.append(cs_); ssm_s.append(hs_)
    y_prompt = _rmsnorm(xp, final_norm_w)
    y_sample = _rmsnorm(xs, final_norm_w)
    return (y_prompt, y_sample,
            jnp.stack(pool_p), jnp.stack(conv_p), jnp.stack(ssm_p),
            jnp.stack(pool_s), jnp.stack(conv_s), jnp.stack(ssm_s))
```

```python
import functools

import numpy as np
import jax
import jax.numpy as jnp
from jax import lax
from jax.experimental import pallas as pl
from jax.experimental.pallas import tpu as pltpu

F32 = jnp.float32
BF16 = jnp.bfloat16

EPS = 1e-6
POOL_WINDOWS = (2, 4, 8, 16)
POOL_HIST_ROWS = 16
CONV_WIDTH = 4
CONV_HIST_ROWS = 8
N_SSM_GROUPS = 4
SSD_CHUNK = 64
PAST_LEN = 4096
LANES = 128
V7X_VMEM_BYTES = 64 * 1024 * 1024
PIECE_LANES = 32


def _dot(a, b):
    return jnp.dot(a, b, preferred_element_type=F32)


def _sigmoid(x):
    return 1.0 / (1.0 + jnp.exp(-x))


def _silu(x):
    return x * _sigmoid(x)


def _softplus(x):
    return jnp.maximum(x, 0.0) + jnp.log1p(jnp.exp(-jnp.abs(x)))


def _rms(x):
    return x * lax.rsqrt(jnp.mean(x * x, axis=-1, keepdims=True) + EPS)


def _pack3(v):
    hi = v.astype(BF16).astype(F32)
    r1 = v - hi
    mid = r1.astype(BF16).astype(F32)
    lo = (r1 - mid).astype(BF16).astype(F32)
    packed = hi + pltpu.roll(mid, PIECE_LANES, axis=1) + pltpu.roll(lo, 2 * PIECE_LANES, axis=1)
    return packed.astype(BF16)


def _adaln_kernel(c_ref, w_ref, b_ref, o_ref):
    c = c_ref[...]
    o_ref[...] = _dot(_silu(c).astype(BF16), w_ref[...].astype(BF16)) + b_ref[...]


def _adaln(c_all, w_ada, b_ada):
    rows, d = c_all.shape
    n = w_ada.shape[1]
    tn = 1024
    return pl.pallas_call(
        _adaln_kernel,
        grid=(n // tn,),
        in_specs=[
            pl.BlockSpec((rows, d), lambda j: (0, 0)),
            pl.BlockSpec((d, tn), lambda j: (0, j)),
            pl.BlockSpec((1, tn), lambda j: (0, j)),
        ],
        out_specs=pl.BlockSpec((rows, tn), lambda j: (0, j)),
        out_shape=jax.ShapeDtypeStruct((rows, n), F32),
        compiler_params=pltpu.CompilerParams(dimension_semantics=("arbitrary",)),
        name="adaln",
    )(c_all, w_ada, b_ada.reshape(1, n))


def _mixer_kernel(x_ref, mod_ref, poolh_ref, convh_ref, h0_ref,
                  n1w_ref, wp_ref, wz_ref, wxbc_ref, wdt_ref, wga_ref, wgb_ref,
                  poolw_ref, pscale_ref, convw_ref, convb_ref, dtb_ref, alog_ref,
                  dskip_ref, snw_ref, wa_ref, wb_ref, wout_ref, e3_ref, tri_ref,
                  x1_ref, poolo_ref, convo_ref, hout_ref,
                  pext, cext, hb_scr, aout_scr, xc_scr, acs_scr, acse_scr, dte_scr, y_scr, ht_scr,
                  *, n_seq, seq_tile, n_tiles, pos0):
    S, Lt = n_seq, seq_tile
    D = x_ref.shape[-1]
    d_inner = wz_ref.shape[-1]
    n_heads = d_inner // 64
    gw = d_inner // N_SSM_GROUPS
    d_state = (wxbc_ref.shape[-1] - d_inner) // (2 * N_SSM_GROUPS)
    pool_g = D // len(POOL_WINDOWS)
    t = pl.program_id(1)

    def seq_rows(s):
        return slice(s * Lt, (s + 1) * Lt)

    def load_history():
        for s in range(S):
            pext[s, 0:POOL_HIST_ROWS, :] = poolh_ref[s]
            cext[s, 0:CONV_HIST_ROWS, :] = convh_ref[s]

    if n_tiles == 1:
        load_history()
    else:
        pl.when(t == 0)(load_history)

    for s in range(S):
        m = mod_ref[s]
        h = _rms(x_ref[s]) * n1w_ref[...] * (1.0 + m[:, D:2 * D]) + m[:, 0:D]
        hb_scr[seq_rows(s), :] = h.astype(BF16)
    hb = hb_scr[...]

    p = _dot(hb, wp_ref[...])
    for s in range(S):
        pext[s, POOL_HIST_ROWS:POOL_HIST_ROWS + Lt, :] = p[seq_rows(s), :]
    pos = pos0 + t * Lt + lax.broadcasted_iota(jnp.int32, (Lt, 1), 0)
    for s in range(S):
        outs = []
        for g, w in enumerate(POOL_WINDOWS):
            e = pext[s, :, g * pool_g:(g + 1) * pool_g]
            acc = e
            k = 1
            while k < w:
                acc = acc + pltpu.roll(acc, k, axis=0)
                k *= 2
            cnt = jnp.minimum(pos + 1, w).astype(F32)
            mg = acc[POOL_HIST_ROWS:, :] / cnt - e[POOL_HIST_ROWS:, :]
            outs.append(_dot(mg.astype(BF16), poolw_ref[g]))
        ya = jnp.concatenate(outs, axis=1) * pscale_ref[...]
        aout_scr[seq_rows(s), :] = ya.astype(BF16)
    merged = _sigmoid(_dot(hb, wga_ref[...])) * _dot(aout_scr[...], wa_ref[...])

    xbc = _dot(hb, wxbc_ref[...])
    for s in range(S):
        cext[s, CONV_HIST_ROWS:CONV_HIST_ROWS + Lt, :] = xbc[seq_rows(s), :]
    cblk = 512
    for s in range(S):
        for c0 in range(0, cext.shape[-1], cblk):
            cols = slice(c0, c0 + cblk)
            e = cext[s, :, cols]
            w = convw_ref[:, cols]
            acc = e * w[CONV_WIDTH - 1:CONV_WIDTH, :] + convb_ref[:, cols]
            for k in range(1, CONV_WIDTH):
                acc = acc + pltpu.roll(e, k, axis=0) * w[CONV_WIDTH - 1 - k:CONV_WIDTH - k, :]
            xc_scr[seq_rows(s), cols] = _silu(acc[CONV_HIST_ROWS:, :])

    lane = lax.broadcasted_iota(jnp.int32, (1, LANES), 1)
    dt = _softplus(_dot(hb, wdt_ref[...]) + dtb_ref[...])
    dt = jnp.where(lane < n_heads, dt, 0.0)
    a = dt * (-jnp.exp(alog_ref[...]))
    acs = jnp.dot(tri_ref[...], a, precision=lax.Precision.HIGHEST, preferred_element_type=F32)
    acs_scr[...] = acs
    acse_scr[...] = _dot(_pack3(acs), e3_ref[...])
    dte_scr[...] = _dot(_pack3(dt), e3_ref[...])

    row64 = lax.broadcasted_iota(jnp.int32, (SSD_CHUNK, LANES), 0)
    lane64 = lax.broadcasted_iota(jnp.int32, (SSD_CHUNK, LANES), 1)
    causal2 = row64 >= (lane64 & (SSD_CHUNK - 1))
    left = lane64 < SSD_CHUNK
    left_row = lane < SSD_CHUNK
    pairs_per_group = gw // LANES
    bc0 = d_inner
    cc0 = d_inner + N_SSM_GROUPS * d_state

    def chunk_body(s, c):
        r0 = pl.multiple_of(s * Lt + c * SSD_CHUNK, SSD_CHUNK)
        rows = pl.ds(r0, SSD_CHUNK)
        acs_c = acs_scr[rows, :]
        acs_t2 = jnp.concatenate([acs_c, acs_c], axis=0).T
        acse_c = acse_scr[rows, :]
        last = acse_scr[pl.ds(r0 + SSD_CHUNK - 1, 1), :]
        xdt = xc_scr[rows, 0:d_inner] * dte_scr[rows, :]
        xw_b = (xdt * jnp.exp(last - acse_c)).astype(BF16)
        e_acs = jnp.exp(acse_c)
        e_last = jnp.exp(last)
        for g in range(N_SSM_GROUPS):
            gl = slice(g * gw, (g + 1) * gw)
            bm = xc_scr[rows, bc0 + g * d_state:bc0 + (g + 1) * d_state].astype(BF16)
            cm = xc_scr[rows, cc0 + g * d_state:cc0 + (g + 1) * d_state].astype(BF16)
            bm2 = jnp.concatenate([bm, bm], axis=0)
            cb2 = lax.dot_general(cm, bm2, (((1,), (1,)), ((), ())), preferred_element_type=F32)
            ht = ht_scr[g]
            yoff = _dot(cm, ht.astype(BF16)) * e_acs[:, gl]
            for k in range(pairs_per_group):
                l0 = g * gw + k * LANES
                hd = (l0 // 64)
                col = acse_c[:, l0:l0 + LANES]
                rowp = jnp.where(left_row, acs_t2[hd:hd + 1, :], acs_t2[hd + 1:hd + 2, :])
                decay = jnp.exp(jnp.where(causal2, col - rowp, -jnp.inf))
                m2 = (cb2 * decay).astype(BF16)
                x2 = xdt[:, l0:l0 + LANES]
                bd = jnp.concatenate([jnp.where(left, x2, 0.0), jnp.where(left, 0.0, x2)], axis=0)
                y_scr[rows, l0:l0 + LANES] = _dot(m2, bd.astype(BF16)) + yoff[:, k * LANES:(k + 1) * LANES]
            st = lax.dot_general(bm, xw_b[:, gl], (((0,), (0,)), ((), ())), preferred_element_type=F32)
            ht_scr[g] = ht * e_last[:, gl] + st

    def load_state(s):
        for g in range(N_SSM_GROUPS):
            ht_scr[g] = h0_ref[s, g].T

    def store_state(s):
        for g in range(N_SSM_GROUPS):
            hout_ref[s, g] = ht_scr[g].T

    for s in range(S):
        if n_tiles == 1:
            load_state(s)
        else:
            pl.when(t == 0)(functools.partial(load_state, s))

        def body(c, carry, s=s):
            chunk_body(s, c)
            return carry

        lax.fori_loop(0, Lt // SSD_CHUNK, body, 0)
        if n_tiles == 1:
            store_state(s)
        else:
            pl.when(t == n_tiles - 1)(functools.partial(store_state, s))

    y = y_scr[...] + dskip_ref[...] * xc_scr[:, 0:d_inner]
    y = y * _silu(_dot(hb, wz_ref[...]))
    yn = jnp.concatenate([_rms(y[:, g * gw:(g + 1) * gw]) for g in range(N_SSM_GROUPS)], axis=1)
    bout = (yn * snw_ref[...]).astype(BF16)
    merged = merged + _sigmoid(_dot(hb, wgb_ref[...])) * _dot(bout, wb_ref[...])
    o = _dot(merged.astype(BF16), wout_ref[...])
    for s in range(S):
        x1_ref[s] = x_ref[s] + mod_ref[s][:, 2 * D:3 * D] * o[seq_rows(s), :]

    for s in range(S):
        poolo_ref[s] = pext[s, Lt:Lt + POOL_HIST_ROWS, :]
        convo_ref[s] = cext[s, Lt:Lt + CONV_HIST_ROWS, :]
    if n_tiles > 1:
        for s in range(S):
            pext[s, 0:POOL_HIST_ROWS, :] = pext[s, Lt:Lt + POOL_HIST_ROWS, :]
            cext[s, 0:CONV_HIST_ROWS, :] = cext[s, Lt:Lt + CONV_HIST_ROWS, :]


def _const_spec(arr):
    nd = arr.ndim
    return pl.BlockSpec(arr.shape, lambda b, t, _nd=nd: (0,) * _nd, pipeline_mode=pl.Buffered(1))


def _mixer(x, mod, pool_hist, conv_hist, h0, wts, *, n_seq, seq_tile, pos0):
    B, L, D = x.shape
    S, Lt = n_seq, seq_tile
    assert B % S == 0 and L % Lt == 0 and Lt % SSD_CHUNK == 0
    n_tiles = L // Lt
    assert S == 1 or n_tiles == 1
    T = S * Lt
    d_inner = wts["wz"].shape[1]
    conv_dim = wts["wxbc"].shape[1]
    gw = d_inner // N_SSM_GROUPS
    d_state = h0.shape[-1]

    consts = [wts[k] for k in ("n1w", "wp", "wz", "wxbc", "wdt", "wga", "wgb", "poolw", "pscale",
                               "convw", "convb", "dtb", "alog", "dskip", "snw", "wa", "wb", "wout", "e3")]
    ci = np.arange(T) // SSD_CHUNK
    tri = jnp.asarray((ci[:, None] == ci[None, :]) & (np.arange(T)[None, :] <= np.arange(T)[:, None]), F32)
    consts.append(tri)

    in_specs = [
        pl.BlockSpec((S, Lt, D), lambda b, t: (b, t, 0)),
        pl.BlockSpec((S, 1, mod.shape[-1]), lambda b, t: (b, 0, 0)),
        pl.BlockSpec((S, POOL_HIST_ROWS, D), lambda b, t: (b, 0, 0)),
        pl.BlockSpec((S, CONV_HIST_ROWS, conv_dim), lambda b, t: (b, 0, 0)),
        pl.BlockSpec((S, N_SSM_GROUPS, gw, d_state), lambda b, t: (b, 0, 0, 0)),
    ] + [_const_spec(c) for c in consts]
    out_shape = [
        jax.ShapeDtypeStruct((B, L, D), F32),
        jax.ShapeDtypeStruct((B, POOL_HIST_ROWS, D), F32),
        jax.ShapeDtypeStruct((B, CONV_HIST_ROWS, conv_dim), F32),
        jax.ShapeDtypeStruct((B, N_SSM_GROUPS, gw, d_state), F32),
    ]
    out_specs = [
        pl.BlockSpec((S, Lt, D), lambda b, t: (b, t, 0)),
        pl.BlockSpec((S, POOL_HIST_ROWS, D), lambda b, t: (b, 0, 0)),
        pl.BlockSpec((S, CONV_HIST_ROWS, conv_dim), lambda b, t: (b, 0, 0)),
        pl.BlockSpec((S, N_SSM_GROUPS, gw, d_state), lambda b, t: (b, 0, 0, 0)),
    ]
    scratch = [
        pltpu.VMEM((S, POOL_HIST_ROWS + Lt, D), F32),
        pltpu.VMEM((S, CONV_HIST_ROWS + Lt, conv_dim), F32),
        pltpu.VMEM((T, D), BF16),
        pltpu.VMEM((T, D), BF16),
        pltpu.VMEM((T, conv_dim), F32),
        pltpu.VMEM((T, LANES), F32),
        pltpu.VMEM((T, d_inner), F32),
        pltpu.VMEM((T, d_inner), F32),
        pltpu.VMEM((T, d_inner), F32),
        pltpu.VMEM((N_SSM_GROUPS, d_state, gw), F32),
    ]
    kern = functools.partial(_mixer_kernel, n_seq=S, seq_tile=Lt, n_tiles=n_tiles, pos0=pos0)
    return pl.pallas_call(
        kern,
        grid=(B // S, n_tiles),
        in_specs=in_specs,
        out_specs=out_specs,
        out_shape=out_shape,
        scratch_shapes=scratch,
        compiler_params=pltpu.CompilerParams(
            dimension_semantics=("arbitrary", "arbitrary"),
            vmem_limit_bytes=V7X_VMEM_BYTES - 4 * 1024 * 1024,
        ),
        name="mixer",
    )(x, mod, pool_hist, conv_hist, h0, *consts)


def _ffn_kernel(x_ref, mod_ref, n2w_ref, wg_ref, wu_ref, wd_ref, fnw_ref, y_ref, hb_scr,
                *, n_seq, seq_tile, final_norm):
    S, Lt = n_seq, seq_tile
    D = x_ref.shape[-1]
    for s in range(S):
        m = mod_ref[s]
        h = _rms(x_ref[s]) * n2w_ref[...] * (1.0 + m[:, 4 * D:5 * D]) + m[:, 3 * D:4 * D]
        hb_scr[s * Lt:(s + 1) * Lt, :] = h.astype(BF16)
    hb = hb_scr[...]
    act = (_silu(_dot(hb, wg_ref[...])) * _dot(hb, wu_ref[...])).astype(BF16)
    f = _dot(act, wd_ref[...])
    for s in range(S):
        x2 = x_ref[s] + mod_ref[s][:, 5 * D:6 * D] * f[s * Lt:(s + 1) * Lt, :]
        y_ref[s] = _rms(x2) * fnw_ref[...] if final_norm else x2


def _ffn(x, mod, wts, *, n_seq, seq_tile, final_norm):
    B, L, D = x.shape
    S, Lt = n_seq, seq_tile
    assert B % S == 0 and L % Lt == 0
    consts = [wts[k] for k in ("n2w", "wg", "wu", "wd", "fnw")]
    kern = functools.partial(_ffn_kernel, n_seq=S, seq_tile=Lt, final_norm=final_norm)
    return pl.pallas_call(
        kern,
        grid=(B // S, L // Lt),
        in_specs=[
            pl.BlockSpec((S, Lt, D), lambda b, t: (b, t, 0)),
            pl.BlockSpec((S, 1, mod.shape[-1]), lambda b, t: (b, 0, 0)),
        ] + [_const_spec(c) for c in consts],
        out_specs=pl.BlockSpec((S, Lt, D), lambda b, t: (b, t, 0)),
        out_shape=jax.ShapeDtypeStruct((B, L, D), F32),
        scratch_shapes=[pltpu.VMEM((S * Lt, D), BF16)],
        compiler_params=pltpu.CompilerParams(
            dimension_semantics=("arbitrary", "arbitrary"),
            vmem_limit_bytes=V7X_VMEM_BYTES - 8 * 1024 * 1024,
        ),
        name="ffn",
    )(x, mod, *consts)


def _expand_matrix(n_heads, head_dim):
    k = np.arange(LANES)[:, None]
    j = np.arange(n_heads * head_dim)[None, :]
    return jnp.asarray((k < 3 * PIECE_LANES) & ((k % PIECE_LANES) == (j // head_dim)), BF16)


def _layer_weights(l, w_in, norm1_w, norm2_w, pool_group_w, pool_scale, conv_w, conv_b, dt_bias, a_log,
                   d_skip, ssm_norm_w, w_branch_a, w_branch_b, w_out, w_ffn_gate, w_ffn_up, w_ffn_down,
                   final_norm_w):
    D = w_in.shape[1]
    d_inner = w_branch_b.shape[1]
    conv_dim = conv_w.shape[-1]
    n_heads = a_log.shape[-1]
    assert n_heads == PIECE_LANES
    head_dim = d_inner // n_heads
    c0, c1, c2, c3, c4 = D, D + d_inner, D + d_inner + conv_dim, D + d_inner + conv_dim + n_heads, \
        2 * D + d_inner + conv_dim + n_heads
    wi = w_in[l]
    row = lambda v: v.reshape(1, -1).astype(F32)
    pad_heads = lambda v: jnp.pad(row(v), ((0, 0), (0, LANES - n_heads)))
    return dict(
        n1w=row(norm1_w[l]), n2w=row(norm2_w[l]), fnw=row(final_norm_w),
        wp=wi[:, :c0].astype(BF16), wz=wi[:, c0:c1].astype(BF16), wxbc=wi[:, c1:c2].astype(BF16),
        wdt=jnp.pad(wi[:, c2:c3], ((0, 0), (0, LANES - n_heads))).astype(BF16),
        wga=wi[:, c3:c4].astype(BF16), wgb=wi[:, c4:].astype(BF16),
        poolw=pool_group_w[l].astype(BF16), pscale=row(pool_scale[l]),
        convw=conv_w[l].astype(F32), convb=row(conv_b[l]),
        dtb=pad_heads(dt_bias[l]), alog=pad_heads(a_log[l]),
        dskip=row(jnp.repeat(d_skip[l], head_dim)), snw=row(ssm_norm_w[l]),
        wa=w_branch_a[l].astype(BF16), wb=w_branch_b[l].astype(BF16), wout=w_out[l].astype(BF16),
        wg=w_ffn_gate[l].astype(BF16), wu=w_ffn_up[l].astype(BF16), wd=w_ffn_down[l].astype(BF16),
        e3=_expand_matrix(n_heads, head_dim),
    )


def _pad_hist(h, rows):
    return jnp.pad(h, ((0, 0), (rows - h.shape[1], 0), (0, 0)))


def kernel(x_prompt, x_sample, c_prompt, c_sample, state_pool, state_conv, state_ssm, w_ada, b_ada, norm1_w, norm2_w, w_in, pool_group_w, pool_scale, conv_w, conv_b, dt_bias, a_log, d_skip, ssm_norm_w, w_branch_a, w_branch_b, w_out, w_ffn_gate, w_ffn_up, w_ffn_down, final_norm_w):
    depth = w_in.shape[0]
    bp, lp, D = x_prompt.shape
    bs, ls, _ = x_sample.shape
    conv_dim = conv_w.shape[-1]
    _, _, n_heads, head_dim, d_state = state_ssm.shape
    gw = n_heads * head_dim // N_SSM_GROUPS
    pool_rows = state_pool.shape[2]
    conv_rows = state_conv.shape[2]

    c_all = jnp.concatenate([c_prompt, c_sample], axis=0)
    c_all = jnp.pad(c_all, ((0, (-c_all.shape[0]) % 8), (0, 0)))

    zeros_pool = jnp.zeros((bp, POOL_HIST_ROWS, D), F32)
    zeros_conv = jnp.zeros((bp, CONV_HIST_ROWS, conv_dim), F32)
    zeros_ssm = jnp.zeros((bp, N_SSM_GROUPS, gw, d_state), F32)

    xp, xs = x_prompt, x_sample
    outs = [[] for _ in range(6)]
    for l in range(depth):
        wts = _layer_weights(l, w_in, norm1_w, norm2_w, pool_group_w, pool_scale, conv_w, conv_b, dt_bias,
                             a_log, d_skip, ssm_norm_w, w_branch_a, w_branch_b, w_out, w_ffn_gate,
                             w_ffn_up, w_ffn_down, final_norm_w)
        mod = _adaln(c_all, w_ada[l], b_ada[l])
        mod_p = mod[:bp].reshape(bp, 1, -1)
        mod_s = mod[bp:bp + bs].reshape(bs, 1, -1)
        last = l == depth - 1

        xp, pool_p, conv_p, ssm_p = _mixer(xp, mod_p, zeros_pool, zeros_conv, zeros_ssm, wts,
                                           n_seq=1, seq_tile=256, pos0=0)
        xp = _ffn(xp, mod_p, wts, n_seq=1, seq_tile=512, final_norm=last)

        xs, pool_s, conv_s, ssm_s = _mixer(
            xs, mod_s, _pad_hist(state_pool[l], POOL_HIST_ROWS), _pad_hist(state_conv[l], CONV_HIST_ROWS),
            state_ssm[l].reshape(bs, N_SSM_GROUPS, gw, d_state), wts,
            n_seq=2, seq_tile=ls, pos0=PAST_LEN)
        xs = _ffn(xs, mod_s, wts, n_seq=8, seq_tile=ls, final_norm=last)

        for acc, v in zip(outs, (pool_p[:, POOL_HIST_ROWS - pool_rows:], conv_p[:, CONV_HIST_ROWS - conv_rows:],
                                 ssm_p.reshape(bp, n_heads, head_dim, d_state),
                                 pool_s[:, POOL_HIST_ROWS - pool_rows:], conv_s[:, CONV_HIST_ROWS - conv_rows:],
                                 ssm_s.reshape(bs, n_heads, head_dim, d_state))):
            acc.append(v)
    return (xp, xs) + tuple(jnp.stack(o) for o in outs)
```

```python
import functools

import numpy as np
import jax
import jax.numpy as jnp
from jax import lax
from jax.experimental import pallas as pl
from jax.experimental.pallas import tpu as pltpu

F32 = jnp.float32
BF16 = jnp.bfloat16

EPS = 1e-6
POOL_WINDOWS = (2, 4, 8, 16)
POOL_HIST_ROWS = 16
CONV_WIDTH = 4
CONV_HIST_ROWS = 8
N_SSM_GROUPS = 4
SSD_CHUNK = 64
PAST_LEN = 4096
LANES = 128
V7X_VMEM_BYTES = 64 * 1024 * 1024
PIECE_LANES = 32
COL_BLOCK = 512


def _dot(a, b):
    return jnp.dot(a, b, preferred_element_type=F32)


def _sigmoid(x):
    return 1.0 / (1.0 + jnp.exp(-x))


def _silu(x):
    return x * _sigmoid(x)


def _softplus(x):
    return jnp.maximum(x, 0.0) + jnp.log1p(jnp.exp(-jnp.abs(x)))


def _rms(x):
    return x * lax.rsqrt(jnp.mean(x * x, axis=-1, keepdims=True) + EPS)


def _pack3(v):
    hi = v.astype(BF16).astype(F32)
    r1 = v - hi
    mid = r1.astype(BF16).astype(F32)
    lo = (r1 - mid).astype(BF16).astype(F32)
    packed = hi + pltpu.roll(mid, PIECE_LANES, axis=1) + pltpu.roll(lo, 2 * PIECE_LANES, axis=1)
    return packed.astype(BF16)


def _adaln_kernel(c_ref, w_ref, b_ref, o_ref):
    c = c_ref[...]
    o_ref[...] = _dot(_silu(c).astype(BF16), w_ref[...].astype(BF16)) + b_ref[...]


def _adaln(c_all, w_ada, b_ada):
    rows, d = c_all.shape
    n = w_ada.shape[1]
    tn = 1024
    return pl.pallas_call(
        _adaln_kernel,
        grid=(n // tn,),
        in_specs=[
            pl.BlockSpec((rows, d), lambda j: (0, 0)),
            pl.BlockSpec((d, tn), lambda j: (0, j)),
            pl.BlockSpec((1, tn), lambda j: (0, j)),
        ],
        out_specs=pl.BlockSpec((rows, tn), lambda j: (0, j)),
        out_shape=jax.ShapeDtypeStruct((rows, n), F32),
        compiler_params=pltpu.CompilerParams(dimension_semantics=("arbitrary",)),
        name="adaln",
    )(c_all, w_ada, b_ada.reshape(1, n))


def _mixer_kernel(x_ref, mod_ref, poolh_ref, convh_ref, h0_ref,
                  n1w_ref, win_ref, poolw_ref, pscale_ref, convw_ref, convb_ref, dtb_ref, alog_ref,
                  dskip_ref, snw_ref, wa_ref, wb_ref, wout_ref, e3_ref,
                  x1_ref, poolo_ref, convo_ref, hout_ref,
                  pext, cext, hb_scr, aout_scr, xc_scr, acs_scr, acse_scr, y_scr, zs_scr, ht_scr,
                  *, n_seq, seq_tile, n_tiles, pos0):
    S, Lt = n_seq, seq_tile
    T = S * Lt
    D = x_ref.shape[-1]
    d_inner = dskip_ref.shape[-1]
    conv_dim = convb_ref.shape[-1]
    n_heads = d_inner // 64
    gw = d_inner // N_SSM_GROUPS
    d_state = (conv_dim - d_inner) // (2 * N_SSM_GROUPS)
    pool_g = D // len(POOL_WINDOWS)
    n_slabs = conv_dim // LANES
    c_z, c_x, c_dt = D, D + d_inner, D + d_inner + conv_dim
    c_ga, c_gb = c_dt + LANES, c_dt + LANES + D
    t = pl.program_id(1)

    def seq_rows(s):
        return slice(s * Lt, (s + 1) * Lt)

    def load_history():
        for s in range(S):
            pext[s, 0:POOL_HIST_ROWS, :] = poolh_ref[s]
            for j in range(n_slabs):
                cext[s, j, 0:CONV_HIST_ROWS, :] = convh_ref[s, :, j * LANES:(j + 1) * LANES]

    def load_state(s):
        for g in range(N_SSM_GROUPS):
            ht_scr[g] = h0_ref[s, g].T

    def store_state(s):
        for g in range(N_SSM_GROUPS):
            hout_ref[s, g] = ht_scr[g].T

    if n_tiles == 1:
        load_history()
    else:
        @pl.when(t == 0)
        def _():
            load_history()
            load_state(0)

    for s in range(S):
        m = mod_ref[s]
        h = _rms(x_ref[s]) * n1w_ref[...] * (1.0 + m[:, D:2 * D]) + m[:, 0:D]
        hb_scr[seq_rows(s), :] = h.astype(BF16)
    hb = hb_scr[...]

    lane = lax.broadcasted_iota(jnp.int32, (1, LANES), 1)
    dt = _softplus(_dot(hb, win_ref[:, c_dt:c_dt + LANES]) + dtb_ref[...])
    dt = jnp.where(lane < n_heads, dt, 0.0)
    acs = dt * (-jnp.exp(alog_ref[...]))
    row_in_chunk = lax.broadcasted_iota(jnp.int32, (T, 1), 0) & (SSD_CHUNK - 1)
    k = 1
    while k < SSD_CHUNK:
        acs = acs + jnp.where(row_in_chunk >= k, pltpu.roll(acs, k, axis=0), 0.0)
        k *= 2
    acs_scr[...] = acs
    acs_p = _pack3(acs)
    dt_p = _pack3(dt)

    p = _dot(hb, win_ref[:, 0:D])
    for s in range(S):
        pext[s, POOL_HIST_ROWS:POOL_HIST_ROWS + Lt, :] = p[seq_rows(s), :]
    pos = pos0 + t * Lt + lax.broadcasted_iota(jnp.int32, (Lt, 1), 0)
    for s in range(S):
        outs = []
        for g, w in enumerate(POOL_WINDOWS):
            e = pext[s, :, g * pool_g:(g + 1) * pool_g]
            acc = e
            k = 1
            while k < w:
                acc = acc + pltpu.roll(acc, k, axis=0)
                k *= 2
            cnt = jnp.minimum(pos + 1, w).astype(F32)
            mg = acc[POOL_HIST_ROWS:, :] / cnt - e[POOL_HIST_ROWS:, :]
            outs.append(_dot(mg.astype(BF16), poolw_ref[g]))
        ya = jnp.concatenate(outs, axis=1) * pscale_ref[...]
        aout_scr[seq_rows(s), :] = ya.astype(BF16)
    merged = _sigmoid(_dot(hb, win_ref[:, c_ga:c_ga + D])) * _dot(aout_scr[...], wa_ref[...])

    slabs_per_block = COL_BLOCK // LANES
    for j in range(conv_dim // COL_BLOCK):
        xb = _dot(hb, win_ref[:, c_x + j * COL_BLOCK:c_x + (j + 1) * COL_BLOCK])
        is_x = (j + 1) * COL_BLOCK <= d_inner
        if is_x:
            ecols = slice(j * COL_BLOCK, (j + 1) * COL_BLOCK)
            acse_scr[:, ecols] = _dot(acs_p, e3_ref[:, ecols])
            dtx = _dot(dt_p, e3_ref[:, ecols])
        for s in range(S):
            for q in range(slabs_per_block):
                sl = j * slabs_per_block + q
                lc = slice(sl * LANES, (sl + 1) * LANES)
                cext[s, sl, CONV_HIST_ROWS:CONV_HIST_ROWS + Lt, :] = xb[seq_rows(s), q * LANES:(q + 1) * LANES]
                w = convw_ref[:, lc]
                acc = cext[s, sl, CONV_HIST_ROWS:CONV_HIST_ROWS + Lt, :] * w[CONV_WIDTH - 1:CONV_WIDTH, :]
                acc = acc + convb_ref[:, lc]
                for k in range(1, CONV_WIDTH):
                    tap = cext[s, sl, pl.ds(CONV_HIST_ROWS - k, Lt), :]
                    acc = acc + tap * w[CONV_WIDTH - 1 - k:CONV_WIDTH - k, :]
                xc = _silu(acc)
                if is_x:
                    y_scr[seq_rows(s), lc] = dskip_ref[:, lc] * xc
                    xc = xc * dtx[seq_rows(s), q * LANES:(q + 1) * LANES]
                xc_scr[seq_rows(s), lc] = xc

    row64 = lax.broadcasted_iota(jnp.int32, (SSD_CHUNK, LANES), 0)
    lane64 = lax.broadcasted_iota(jnp.int32, (SSD_CHUNK, LANES), 1)
    causal2 = row64 >= (lane64 & (SSD_CHUNK - 1))
    left = lane64 < SSD_CHUNK
    left_row = lane < SSD_CHUNK
    pairs_per_group = gw // LANES
    bc0 = d_inner
    cc0 = d_inner + N_SSM_GROUPS * d_state

    def chunk_body(s, c):
        r0 = s * Lt + c * SSD_CHUNK
        rows = pl.ds(r0, SSD_CHUNK)
        acs_c = acs_scr[rows, :]
        acs_t2 = jnp.concatenate([acs_c, acs_c], axis=0).T
        acse_c = acse_scr[rows, :]
        last = acse_scr[pl.ds(r0 + SSD_CHUNK - 1, 1), :]
        xdt = xc_scr[rows, 0:d_inner]
        xw_b = (xdt * jnp.exp(last - acse_c)).astype(BF16)
        e_acs = jnp.exp(acse_c)
        e_last = jnp.exp(last)
        for g in range(N_SSM_GROUPS):
            gl = slice(g * gw, (g + 1) * gw)
            bm = xc_scr[rows, bc0 + g * d_state:bc0 + (g + 1) * d_state].astype(BF16)
            cm = xc_scr[rows, cc0 + g * d_state:cc0 + (g + 1) * d_state].astype(BF16)
            bm2 = jnp.concatenate([bm, bm], axis=0)
            cb2 = lax.dot_general(cm, bm2, (((1,), (1,)), ((), ())), preferred_element_type=F32)
            ht = ht_scr[g]
            yoff = _dot(cm, ht.astype(BF16)) * e_acs[:, gl]
            for k in range(pairs_per_group):
                l0 = g * gw + k * LANES
                hd = l0 // 64
                col = acse_c[:, l0:l0 + LANES]
                rowp = jnp.where(left_row, acs_t2[hd:hd + 1, :], acs_t2[hd + 1:hd + 2, :])
                decay = jnp.exp(jnp.where(causal2, col - rowp, -jnp.inf))
                m2 = (cb2 * decay).astype(BF16)
                x2 = xdt[:, l0:l0 + LANES]
                bd = jnp.concatenate([jnp.where(left, x2, 0.0), jnp.where(left, 0.0, x2)], axis=0)
                yc = _dot(m2, bd.astype(BF16)) + yoff[:, k * LANES:(k + 1) * LANES]
                y_scr[rows, l0:l0 + LANES] = y_scr[rows, l0:l0 + LANES] + yc
            st = lax.dot_general(bm, xw_b[:, gl], (((0,), (0,)), ((), ())), preferred_element_type=F32)
            ht_scr[g] = ht * e_last[:, gl] + st

    n_zblk = d_inner // COL_BLOCK

    def z_block(i):
        zc = slice(i * COL_BLOCK, (i + 1) * COL_BLOCK)
        zs_scr[:, zc] = _silu(_dot(hb, win_ref[:, c_z + i * COL_BLOCK:c_z + (i + 1) * COL_BLOCK]))

    z_done = 0
    for s in range(S):
        if n_tiles == 1:
            load_state(s)
        for c in range(Lt // SSD_CHUNK):
            chunk_body(s, c)
            if z_done < n_zblk:
                z_block(z_done)
                z_done += 1
        if n_tiles == 1:
            store_state(s)
    for i in range(z_done, n_zblk):
        z_block(i)
    sgb = _sigmoid(_dot(hb, win_ref[:, c_gb:c_gb + D]))

    y = y_scr[...] * zs_scr[...]
    yn = jnp.concatenate([_rms(y[:, g * gw:(g + 1) * gw]) for g in range(N_SSM_GROUPS)], axis=1)
    bout = (yn * snw_ref[...]).astype(BF16)
    merged = merged + sgb * _dot(bout, wb_ref[...])
    o = _dot(merged.astype(BF16), wout_ref[...])
    for s in range(S):
        x1_ref[s] = x_ref[s] + mod_ref[s][:, 2 * D:3 * D] * o[seq_rows(s), :]

    for s in range(S):
        poolo_ref[s] = pext[s, Lt:Lt + POOL_HIST_ROWS, :]
        for j in range(n_slabs):
            convo_ref[s, :, j * LANES:(j + 1) * LANES] = cext[s, j, Lt:Lt + CONV_HIST_ROWS, :]
    if n_tiles > 1:
        pl.when(t == n_tiles - 1)(functools.partial(store_state, 0))
        for s in range(S):
            pext[s, 0:POOL_HIST_ROWS, :] = pext[s, Lt:Lt + POOL_HIST_ROWS, :]
            for j in range(n_slabs):
                cext[s, j, 0:CONV_HIST_ROWS, :] = cext[s, j, Lt:Lt + CONV_HIST_ROWS, :]


def _const_spec(arr):
    nd = arr.ndim
    return pl.BlockSpec(arr.shape, lambda b, t, _nd=nd: (0,) * _nd, pipeline_mode=pl.Buffered(1))


def _mixer(x, mod, pool_hist, conv_hist, h0, wts, *, n_seq, seq_tile, pos0):
    B, L, D = x.shape
    S, Lt = n_seq, seq_tile
    assert B % S == 0 and L % Lt == 0 and Lt % SSD_CHUNK == 0
    n_tiles = L // Lt
    assert S == 1 or n_tiles == 1
    T = S * Lt
    d_inner = wts["dskip"].shape[1]
    conv_dim = wts["convb"].shape[1]
    assert d_inner % COL_BLOCK == 0 and conv_dim % COL_BLOCK == 0
    gw = d_inner // N_SSM_GROUPS
    d_state = h0.shape[-1]

    consts = [wts[k] for k in ("n1w", "win", "poolw", "pscale", "convw", "convb", "dtb", "alog",
                               "dskip", "snw", "wa", "wb", "wout", "e3")]
    in_specs = [
        pl.BlockSpec((S, Lt, D), lambda b, t: (b, t, 0)),
        pl.BlockSpec((S, 1, mod.shape[-1]), lambda b, t: (b, 0, 0)),
        pl.BlockSpec((S, POOL_HIST_ROWS, D), lambda b, t: (b, 0, 0)),
        pl.BlockSpec((S, CONV_HIST_ROWS, conv_dim), lambda b, t: (b, 0, 0)),
        pl.BlockSpec((S, N_SSM_GROUPS, gw, d_state), lambda b, t: (b, 0, 0, 0)),
    ] + [_const_spec(c) for c in consts]
    out_shape = [
        jax.ShapeDtypeStruct((B, L, D), F32),
        jax.ShapeDtypeStruct((B, POOL_HIST_ROWS, D), F32),
        jax.ShapeDtypeStruct((B, CONV_HIST_ROWS, conv_dim), F32),
        jax.ShapeDtypeStruct((B, N_SSM_GROUPS, gw, d_state), F32),
    ]
    out_specs = [
        pl.BlockSpec((S, Lt, D), lambda b, t: (b, t, 0)),
        pl.BlockSpec((S, POOL_HIST_ROWS, D), lambda b, t: (b, 0, 0)),
        pl.BlockSpec((S, CONV_HIST_ROWS, conv_dim), lambda b, t: (b, 0, 0)),
        pl.BlockSpec((S, N_SSM_GROUPS, gw, d_state), lambda b, t: (b, 0, 0, 0)),
    ]
    scratch = [
        pltpu.VMEM((S, POOL_HIST_ROWS + Lt, D), F32),
        pltpu.VMEM((S, conv_dim // LANES, CONV_HIST_ROWS + Lt, LANES), F32),
        pltpu.VMEM((T, D), BF16),
        pltpu.VMEM((T, D), BF16),
        pltpu.VMEM((T, conv_dim), F32),
        pltpu.VMEM((T, LANES), F32),
        pltpu.VMEM((T, d_inner), F32),
        pltpu.VMEM((T, d_inner), F32),
        pltpu.VMEM((T, d_inner), F32),
        pltpu.VMEM((N_SSM_GROUPS, d_state, gw), F32),
    ]
    kern = functools.partial(_mixer_kernel, n_seq=S, seq_tile=Lt, n_tiles=n_tiles, pos0=pos0)
    return pl.pallas_call(
        kern,
        grid=(B // S, n_tiles),
        in_specs=in_specs,
        out_specs=out_specs,
        out_shape=out_shape,
        scratch_shapes=scratch,
        compiler_params=pltpu.CompilerParams(
            dimension_semantics=("arbitrary", "arbitrary"),
            vmem_limit_bytes=V7X_VMEM_BYTES - 4 * 1024 * 1024,
        ),
        name="mixer",
    )(x, mod, pool_hist, conv_hist, h0, *consts)


def _ffn_kernel(x_ref, mod_ref, n2w_ref, wg_ref, wu_ref, wd_ref, fnw_ref, y_ref, hb_scr,
                *, n_seq, seq_tile, final_norm):
    S, Lt = n_seq, seq_tile
    D = x_ref.shape[-1]
    for s in range(S):
        m = mod_ref[s]
        h = _rms(x_ref[s]) * n2w_ref[...] * (1.0 + m[:, 4 * D:5 * D]) + m[:, 3 * D:4 * D]
        hb_scr[s * Lt:(s + 1) * Lt, :] = h.astype(BF16)
    hb = hb_scr[...]
    act = (_silu(_dot(hb, wg_ref[...])) * _dot(hb, wu_ref[...])).astype(BF16)
    f = _dot(act, wd_ref[...])
    for s in range(S):
        x2 = x_ref[s] + mod_ref[s][:, 5 * D:6 * D] * f[s * Lt:(s + 1) * Lt, :]
        y_ref[s] = _rms(x2) * fnw_ref[...] if final_norm else x2


def _ffn(x, mod, wts, *, n_seq, seq_tile, final_norm):
    B, L, D = x.shape
    S, Lt = n_seq, seq_tile
    assert B % S == 0 and L % Lt == 0
    consts = [wts[k] for k in ("n2w", "wg", "wu", "wd", "fnw")]
    kern = functools.partial(_ffn_kernel, n_seq=S, seq_tile=Lt, final_norm=final_norm)
    return pl.pallas_call(
        kern,
        grid=(B // S, L // Lt),
        in_specs=[
            pl.BlockSpec((S, Lt, D), lambda b, t: (b, t, 0)),
            pl.BlockSpec((S, 1, mod.shape[-1]), lambda b, t: (b, 0, 0)),
        ] + [_const_spec(c) for c in consts],
        out_specs=pl.BlockSpec((S, Lt, D), lambda b, t: (b, t, 0)),
        out_shape=jax.ShapeDtypeStruct((B, L, D), F32),
        scratch_shapes=[pltpu.VMEM((S * Lt, D), BF16)],
        compiler_params=pltpu.CompilerParams(
            dimension_semantics=("arbitrary", "arbitrary"),
            vmem_limit_bytes=V7X_VMEM_BYTES - 8 * 1024 * 1024,
        ),
        name="ffn",
    )(x, mod, *consts)


def _expand_matrix(n_heads, head_dim):
    k = np.arange(LANES)[:, None]
    j = np.arange(n_heads * head_dim)[None, :]
    return jnp.asarray((k < 3 * PIECE_LANES) & ((k % PIECE_LANES) == (j // head_dim)), BF16)


def _layer_weights(l, w_in, norm1_w, norm2_w, pool_group_w, pool_scale, conv_w, conv_b, dt_bias, a_log,
                   d_skip, ssm_norm_w, w_branch_a, w_branch_b, w_out, w_ffn_gate, w_ffn_up, w_ffn_down,
                   final_norm_w):
    D = w_in.shape[1]
    d_inner = w_branch_b.shape[1]
    conv_dim = conv_w.shape[-1]
    n_heads = a_log.shape[-1]
    assert n_heads == PIECE_LANES
    head_dim = d_inner // n_heads
    c_dt = D + d_inner + conv_dim
    wi = w_in[l].astype(BF16)
    win = jnp.concatenate([wi[:, :c_dt], jnp.pad(wi[:, c_dt:c_dt + n_heads], ((0, 0), (0, LANES - n_heads))),
                           wi[:, c_dt + n_heads:]], axis=1)
    row = lambda v: v.reshape(1, -1).astype(F32)
    pad_heads = lambda v: jnp.pad(row(v), ((0, 0), (0, LANES - n_heads)))
    return dict(
        n1w=row(norm1_w[l]), n2w=row(norm2_w[l]), fnw=row(final_norm_w), win=win,
        poolw=pool_group_w[l].astype(BF16), pscale=row(pool_scale[l]),
        convw=conv_w[l].astype(F32), convb=row(conv_b[l]),
        dtb=pad_heads(dt_bias[l]), alog=pad_heads(a_log[l]),
        dskip=row(jnp.repeat(d_skip[l], head_dim)), snw=row(ssm_norm_w[l]),
        wa=w_branch_a[l].astype(BF16), wb=w_branch_b[l].astype(BF16), wout=w_out[l].astype(BF16),
        wg=w_ffn_gate[l].astype(BF16), wu=w_ffn_up[l].astype(BF16), wd=w_ffn_down[l].astype(BF16),
        e3=_expand_matrix(n_heads, head_dim),
    )


def _pad_hist(h, rows):
    return jnp.pad(h, ((0, 0), (rows - h.shape[1], 0), (0, 0)))


def kernel(x_prompt, x_sample, c_prompt, c_sample, state_pool, state_conv, state_ssm, w_ada, b_ada, norm1_w, norm2_w, w_in, pool_group_w, pool_scale, conv_w, conv_b, dt_bias, a_log, d_skip, ssm_norm_w, w_branch_a, w_branch_b, w_out, w_ffn_gate, w_ffn_up, w_ffn_down, final_norm_w):
    depth = w_in.shape[0]
    bp, lp, D = x_prompt.shape
    bs, ls, _ = x_sample.shape
    conv_dim = conv_w.shape[-1]
    _, _, n_heads, head_dim, d_state = state_ssm.shape
    gw = n_heads * head_dim // N_SSM_GROUPS
    pool_rows = state_pool.shape[2]
    conv_rows = state_conv.shape[2]

    c_all = jnp.concatenate([c_prompt, c_sample], axis=0)
    c_all = jnp.pad(c_all, ((0, (-c_all.shape[0]) % 8), (0, 0)))

    zeros_pool = jnp.zeros((bp, POOL_HIST_ROWS, D), F32)
    zeros_conv = jnp.zeros((bp, CONV_HIST_ROWS, conv_dim), F32)
    zeros_ssm = jnp.zeros((bp, N_SSM_GROUPS, gw, d_state), F32)

    xp, xs = x_prompt, x_sample
    outs = [[] for _ in range(6)]
    for l in range(depth):
        wts = _layer_weights(l, w_in, norm1_w, norm2_w, pool_group_w, pool_scale, conv_w, conv_b, dt_bias,
                             a_log, d_skip, ssm_norm_w, w_branch_a, w_branch_b, w_out, w_ffn_gate,
                             w_ffn_up, w_ffn_down, final_norm_w)
        mod = _adaln(c_all, w_ada[l], b_ada[l])
        mod_p = mod[:bp].reshape(bp, 1, -1)
        mod_s = mod[bp:bp + bs].reshape(bs, 1, -1)
        last = l == depth - 1

        xp, pool_p, conv_p, ssm_p = _mixer(xp, mod_p, zeros_pool, zeros_conv, zeros_ssm, wts,
                                           n_seq=1, seq_tile=256, pos0=0)
        xp = _ffn(xp, mod_p, wts, n_seq=1, seq_tile=512, final_norm=last)

        xs, pool_s, conv_s, ssm_s = _mixer(
            xs, mod_s, _pad_hist(state_pool[l], POOL_HIST_ROWS), _pad_hist(state_conv[l], CONV_HIST_ROWS),
            state_ssm[l].reshape(bs, N_SSM_GROUPS, gw, d_state), wts,
            n_seq=2, seq_tile=ls, pos0=PAST_LEN)
        xs = _ffn(xs, mod_s, wts, n_seq=8, seq_tile=ls, final_norm=last)

        for acc, v in zip(outs, (pool_p[:, POOL_HIST_ROWS - pool_rows:], conv_p[:, CONV_HIST_ROWS - conv_rows:],
                                 ssm_p.reshape(bp, n_heads, head_dim, d_state),
                                 pool_s[:, POOL_HIST_ROWS - pool_rows:], conv_s[:, CONV_HIST_ROWS - conv_rows:],
                                 ssm_s.reshape(bs, n_heads, head_dim, d_state))):
            acc.append(v)
    return (xp, xs) + tuple(jnp.stack(o) for o in outs)
```

```python
import functools

import numpy as np
import jax
import jax.numpy as jnp
from jax import lax
from jax.experimental import pallas as pl
from jax.experimental.pallas import tpu as pltpu

F32 = jnp.float32
BF16 = jnp.bfloat16

EPS = 1e-6
POOL_WINDOWS = (2, 4, 8, 16)
POOL_HIST_ROWS = 16
CONV_WIDTH = 4
CONV_HIST_ROWS = 8
N_SSM_GROUPS = 4
SSD_CHUNK = 64
PAST_LEN = 4096
LANES = 128
V7X_VMEM_BYTES = 64 * 1024 * 1024
PIECE_LANES = 32
COL_BLOCK = 512


def _dot(a, b):
    return jnp.dot(a, b, preferred_element_type=F32)


def _sigmoid(x):
    return 1.0 / (1.0 + jnp.exp(-x))


def _silu(x):
    return x * _sigmoid(x)


def _softplus(x):
    return jnp.maximum(x, 0.0) + jnp.log1p(jnp.exp(-jnp.abs(x)))


def _rms(x):
    return x * lax.rsqrt(jnp.mean(x * x, axis=-1, keepdims=True) + EPS)


def _pack3(v):
    hi = v.astype(BF16).astype(F32)
    r1 = v - hi
    mid = r1.astype(BF16).astype(F32)
    lo = (r1 - mid).astype(BF16).astype(F32)
    packed = hi + pltpu.roll(mid, PIECE_LANES, axis=1) + pltpu.roll(lo, 2 * PIECE_LANES, axis=1)
    return packed.astype(BF16)


def _adaln_kernel(c_ref, w_ref, b_ref, o_ref):
    c = c_ref[...]
    o_ref[...] = _dot(_silu(c).astype(BF16), w_ref[...].astype(BF16)) + b_ref[...]


def _adaln(c_all, w_ada, b_ada):
    rows, d = c_all.shape
    n = w_ada.shape[1]
    tn = 1024
    return pl.pallas_call(
        _adaln_kernel,
        grid=(n // tn,),
        in_specs=[
            pl.BlockSpec((rows, d), lambda j: (0, 0)),
            pl.BlockSpec((d, tn), lambda j: (0, j)),
            pl.BlockSpec((1, tn), lambda j: (0, j)),
        ],
        out_specs=pl.BlockSpec((rows, tn), lambda j: (0, j)),
        out_shape=jax.ShapeDtypeStruct((rows, n), F32),
        compiler_params=pltpu.CompilerParams(dimension_semantics=("arbitrary",)),
        name="adaln",
    )(c_all, w_ada, b_ada.reshape(1, n))


def _mixer_kernel(x_ref, mod_ref, poolh_ref, convh_ref, h0_ref,
                  n1w_ref, win_ref, poolw_ref, pscale_ref, convw_ref, convb_ref, dtb_ref, alog_ref,
                  dskip_ref, snw_ref, wa_ref, wb_ref, wout_ref, e3_ref,
                  x1_ref, poolo_ref, convo_ref, hout_ref,
                  pext, cext, hb_scr, aout_scr, xc_scr, acs_scr, acse_scr, y_scr, zs_scr, ht_scr, bmt_scr, cb2_scr,
                  *, n_seq, seq_tile, n_tiles, pos0):
    S, Lt = n_seq, seq_tile
    T = S * Lt
    D = x_ref.shape[-1]
    d_inner = dskip_ref.shape[-1]
    conv_dim = convb_ref.shape[-1]
    n_heads = d_inner // 64
    gw = d_inner // N_SSM_GROUPS
    d_state = (conv_dim - d_inner) // (2 * N_SSM_GROUPS)
    pool_g = D // len(POOL_WINDOWS)
    n_slabs = conv_dim // LANES
    c_z, c_x, c_dt = D, D + d_inner, D + d_inner + conv_dim
    c_ga, c_gb = c_dt + LANES, c_dt + LANES + D
    t = pl.program_id(1)

    def seq_rows(s):
        return slice(s * Lt, (s + 1) * Lt)

    def load_history():
        for s in range(S):
            pext[s, 0:POOL_HIST_ROWS, :] = poolh_ref[s]
            for j in range(n_slabs):
                cext[s, j, 0:CONV_HIST_ROWS, :] = convh_ref[s, :, j * LANES:(j + 1) * LANES]

    def load_state(s):
        for g in range(N_SSM_GROUPS):
            ht_scr[g] = h0_ref[s, g].T

    def store_state(s):
        for g in range(N_SSM_GROUPS):
            hout_ref[s, g] = ht_scr[g].T

    if n_tiles == 1:
        load_history()
    else:
        @pl.when(t == 0)
        def _():
            load_history()
            load_state(0)

    for s in range(S):
        m = mod_ref[s]
        h = _rms(x_ref[s]) * n1w_ref[...] * (1.0 + m[:, D:2 * D]) + m[:, 0:D]
        hb_scr[seq_rows(s), :] = h.astype(BF16)
    hb = hb_scr[...]

    lane = lax.broadcasted_iota(jnp.int32, (1, LANES), 1)
    dt = _softplus(_dot(hb, win_ref[:, c_dt:c_dt + LANES]) + dtb_ref[...])
    dt = jnp.where(lane < n_heads, dt, 0.0)
    acs = dt * (-jnp.exp(alog_ref[...]))
    row_in_chunk = lax.broadcasted_iota(jnp.int32, (T, 1), 0) & (SSD_CHUNK - 1)
    k = 1
    while k < SSD_CHUNK:
        acs = acs + jnp.where(row_in_chunk >= k, pltpu.roll(acs, k, axis=0), 0.0)
        k *= 2
    acs_scr[...] = acs
    acs_p = _pack3(acs)
    dt_p = _pack3(dt)

    vals = {}

    def pool_in_stage():
        p = _dot(hb, win_ref[:, 0:D])
        for s in range(S):
            pext[s, POOL_HIST_ROWS:POOL_HIST_ROWS + Lt, :] = p[seq_rows(s), :]

    def pool_stage():
        pos = pos0 + t * Lt + lax.broadcasted_iota(jnp.int32, (Lt, 1), 0)
        for s in range(S):
            outs = []
            for g, w in enumerate(POOL_WINDOWS):
                e = pext[s, :, g * pool_g:(g + 1) * pool_g]
                acc = e
                k = 1
                while k < w:
                    acc = acc + pltpu.roll(acc, k, axis=0)
                    k *= 2
                cnt = jnp.minimum(pos + 1, w).astype(F32)
                mg = acc[POOL_HIST_ROWS:, :] / cnt - e[POOL_HIST_ROWS:, :]
                outs.append(_dot(mg.astype(BF16), poolw_ref[g]))
            ya = jnp.concatenate(outs, axis=1) * pscale_ref[...]
            aout_scr[seq_rows(s), :] = ya.astype(BF16)

    def gate_a_stage():
        vals["merged"] = _sigmoid(_dot(hb, win_ref[:, c_ga:c_ga + D])) * _dot(aout_scr[...], wa_ref[...])

    def gate_b_stage():
        vals["sgb"] = _sigmoid(_dot(hb, win_ref[:, c_gb:c_gb + D]))

    slabs_per_block = COL_BLOCK // LANES
    for j in range(conv_dim // COL_BLOCK):
        xb = _dot(hb, win_ref[:, c_x + j * COL_BLOCK:c_x + (j + 1) * COL_BLOCK])
        is_x = (j + 1) * COL_BLOCK <= d_inner
        if is_x:
            ecols = slice(j * COL_BLOCK, (j + 1) * COL_BLOCK)
            acse_scr[:, ecols] = _dot(acs_p, e3_ref[:, ecols])
            dtx = _dot(dt_p, e3_ref[:, ecols])
        for s in range(S):
            for q in range(slabs_per_block):
                sl = j * slabs_per_block + q
                lc = slice(sl * LANES, (sl + 1) * LANES)
                cext[s, sl, CONV_HIST_ROWS:CONV_HIST_ROWS + Lt, :] = xb[seq_rows(s), q * LANES:(q + 1) * LANES]
                w = convw_ref[:, lc]
                acc = cext[s, sl, CONV_HIST_ROWS:CONV_HIST_ROWS + Lt, :] * w[CONV_WIDTH - 1:CONV_WIDTH, :]
                acc = acc + convb_ref[:, lc]
                for k in range(1, CONV_WIDTH):
                    tap = cext[s, sl, pl.ds(CONV_HIST_ROWS - k, Lt), :]
                    acc = acc + tap * w[CONV_WIDTH - 1 - k:CONV_WIDTH - k, :]
                xc = _silu(acc)
                if is_x:
                    y_scr[seq_rows(s), lc] = dskip_ref[:, lc] * xc
                    xc = xc * dtx[seq_rows(s), q * LANES:(q + 1) * LANES]
                xc_scr[seq_rows(s), lc] = xc
                g_b = sl - d_inner // LANES
                if 0 <= g_b < N_SSM_GROUPS:
                    for c in range(Lt // SSD_CHUNK):
                        blk = xc[c * SSD_CHUNK:(c + 1) * SSD_CHUNK, :]
                        bmt_scr[s * (Lt // SSD_CHUNK) + c, g_b] = blk.T.astype(BF16)

    row64 = lax.broadcasted_iota(jnp.int32, (SSD_CHUNK, LANES), 0)
    lane64 = lax.broadcasted_iota(jnp.int32, (SSD_CHUNK, LANES), 1)
    causal2 = row64 >= (lane64 & (SSD_CHUNK - 1))
    left = lane64 < SSD_CHUNK
    left_row = lane < SSD_CHUNK
    pairs_per_group = gw // LANES
    bc0 = d_inner
    cc0 = d_inner + N_SSM_GROUPS * d_state

    for ci in range(T // SSD_CHUNK):
        rows = pl.ds(ci * SSD_CHUNK, SSD_CHUNK)
        for g in range(N_SSM_GROUPS):
            bm = xc_scr[rows, bc0 + g * d_state:bc0 + (g + 1) * d_state].astype(BF16)
            cm = xc_scr[rows, cc0 + g * d_state:cc0 + (g + 1) * d_state].astype(BF16)
            bm2 = jnp.concatenate([bm, bm], axis=0)
            cb2_scr[ci, g] = lax.dot_general(cm, bm2, (((1,), (1,)), ((), ())), preferred_element_type=F32)

    def chunk_body(s, c):
        r0 = s * Lt + c * SSD_CHUNK
        rows = pl.ds(r0, SSD_CHUNK)
        acs_c = acs_scr[rows, :]
        acs_t2 = jnp.concatenate([acs_c, acs_c], axis=0).T
        acse_c = acse_scr[rows, :]
        last = acse_scr[pl.ds(r0 + SSD_CHUNK - 1, 1), :]
        xdt = xc_scr[rows, 0:d_inner]
        xw_b = (xdt * jnp.exp(last - acse_c)).astype(BF16)
        e_acs = jnp.exp(acse_c)
        e_last = jnp.exp(last)
        for g in range(N_SSM_GROUPS):
            gl = slice(g * gw, (g + 1) * gw)
            cm = xc_scr[rows, cc0 + g * d_state:cc0 + (g + 1) * d_state].astype(BF16)
            cb2 = cb2_scr[r0 // SSD_CHUNK, g]
            ht = ht_scr[g]
            yoff = _dot(cm, ht.astype(BF16)) * e_acs[:, gl]
            for k in range(pairs_per_group):
                l0 = g * gw + k * LANES
                hd = l0 // 64
                col = acse_c[:, l0:l0 + LANES]
                rowp = jnp.where(left_row, acs_t2[hd:hd + 1, :], acs_t2[hd + 1:hd + 2, :])
                decay = jnp.exp(jnp.where(causal2, col - rowp, -jnp.inf))
                m2 = (cb2 * decay).astype(BF16)
                x2 = xdt[:, l0:l0 + LANES]
                bd = jnp.concatenate([jnp.where(left, x2, 0.0), jnp.where(left, 0.0, x2)], axis=0)
                yc = _dot(m2, bd.astype(BF16)) + yoff[:, k * LANES:(k + 1) * LANES]
                y_scr[rows, l0:l0 + LANES] = y_scr[rows, l0:l0 + LANES] + yc
            st = _dot(bmt_scr[s * (Lt // SSD_CHUNK) + c, g], xw_b[:, gl])
            ht_scr[g] = ht * e_last[:, gl] + st

    n_zblk = d_inner // COL_BLOCK

    def z_block(i):
        zc = slice(i * COL_BLOCK, (i + 1) * COL_BLOCK)
        zs_scr[:, zc] = _silu(_dot(hb, win_ref[:, c_z + i * COL_BLOCK:c_z + (i + 1) * COL_BLOCK]))

    stages = [pool_in_stage, pool_stage, gate_a_stage, gate_b_stage]
    fillers = []
    for i in range(max(n_zblk, len(stages))):
        fillers += [functools.partial(z_block, i)] if i < n_zblk else []
        fillers += stages[i:i + 1]
    n_fill, n_chunks = len(fillers), T // SSD_CHUNK
    chunks_done = 0
    for s in range(S):
        if n_tiles == 1:
            load_state(s)
        for c in range(Lt // SSD_CHUNK):
            chunk_body(s, c)
            chunks_done += 1
            while n_fill - len(fillers) < -(-n_fill * chunks_done // n_chunks):
                fillers.pop(0)()
        if n_tiles == 1:
            store_state(s)

    y = y_scr[...] * zs_scr[...]
    yn = jnp.concatenate([_rms(y[:, g * gw:(g + 1) * gw]) for g in range(N_SSM_GROUPS)], axis=1)
    bout = (yn * snw_ref[...]).astype(BF16)
    merged = vals["merged"] + vals["sgb"] * _dot(bout, wb_ref[...])
    o = _dot(merged.astype(BF16), wout_ref[...])
    for s in range(S):
        x1_ref[s] = x_ref[s] + mod_ref[s][:, 2 * D:3 * D] * o[seq_rows(s), :]

    for s in range(S):
        poolo_ref[s] = pext[s, Lt:Lt + POOL_HIST_ROWS, :]
        for j in range(n_slabs):
            convo_ref[s, :, j * LANES:(j + 1) * LANES] = cext[s, j, Lt:Lt + CONV_HIST_ROWS, :]
    if n_tiles > 1:
        pl.when(t == n_tiles - 1)(functools.partial(store_state, 0))
        for s in range(S):
            pext[s, 0:POOL_HIST_ROWS, :] = pext[s, Lt:Lt + POOL_HIST_ROWS, :]
            for j in range(n_slabs):
                cext[s, j, 0:CONV_HIST_ROWS, :] = cext[s, j, Lt:Lt + CONV_HIST_ROWS, :]


def _const_spec(arr):
    nd = arr.ndim
    return pl.BlockSpec(arr.shape, lambda b, t, _nd=nd: (0,) * _nd, pipeline_mode=pl.Buffered(1))


def _mixer(x, mod, pool_hist, conv_hist, h0, wts, *, n_seq, seq_tile, pos0):
    B, L, D = x.shape
    S, Lt = n_seq, seq_tile
    assert B % S == 0 and L % Lt == 0 and Lt % SSD_CHUNK == 0
    n_tiles = L // Lt
    assert S == 1 or n_tiles == 1
    T = S * Lt
    d_inner = wts["dskip"].shape[1]
    conv_dim = wts["convb"].shape[1]
    assert d_inner % COL_BLOCK == 0 and conv_dim % COL_BLOCK == 0
    gw = d_inner // N_SSM_GROUPS
    d_state = h0.shape[-1]

    consts = [wts[k] for k in ("n1w", "win", "poolw", "pscale", "convw", "convb", "dtb", "alog",
                               "dskip", "snw", "wa", "wb", "wout", "e3")]
    in_specs = [
        pl.BlockSpec((S, Lt, D), lambda b, t: (b, t, 0)),
        pl.BlockSpec((S, 1, mod.shape[-1]), lambda b, t: (b, 0, 0)),
        pl.BlockSpec((S, POOL_HIST_ROWS, D), lambda b, t: (b, 0, 0)),
        pl.BlockSpec((S, CONV_HIST_ROWS, conv_dim), lambda b, t: (b, 0, 0)),
        pl.BlockSpec((S, N_SSM_GROUPS, gw, d_state), lambda b, t: (b, 0, 0, 0)),
    ] + [_const_spec(c) for c in consts]
    out_shape = [
        jax.ShapeDtypeStruct((B, L, D), F32),
        jax.ShapeDtypeStruct((B, POOL_HIST_ROWS, D), F32),
        jax.ShapeDtypeStruct((B, CONV_HIST_ROWS, conv_dim), F32),
        jax.ShapeDtypeStruct((B, N_SSM_GROUPS, gw, d_state), F32),
    ]
    out_specs = [
        pl.BlockSpec((S, Lt, D), lambda b, t: (b, t, 0)),
        pl.BlockSpec((S, POOL_HIST_ROWS, D), lambda b, t: (b, 0, 0)),
        pl.BlockSpec((S, CONV_HIST_ROWS, conv_dim), lambda b, t: (b, 0, 0)),
        pl.BlockSpec((S, N_SSM_GROUPS, gw, d_state), lambda b, t: (b, 0, 0, 0)),
    ]
    scratch = [
        pltpu.VMEM((S, POOL_HIST_ROWS + Lt, D), F32),
        pltpu.VMEM((S, conv_dim // LANES, CONV_HIST_ROWS + Lt, LANES), F32),
        pltpu.VMEM((T, D), BF16),
        pltpu.VMEM((T, D), BF16),
        pltpu.VMEM((T, conv_dim), F32),
        pltpu.VMEM((T, LANES), F32),
        pltpu.VMEM((T, d_inner), F32),
        pltpu.VMEM((T, d_inner), F32),
        pltpu.VMEM((T, d_inner), F32),
        pltpu.VMEM((N_SSM_GROUPS, d_state, gw), F32),
        pltpu.VMEM((T // SSD_CHUNK, N_SSM_GROUPS, d_state, SSD_CHUNK), BF16),
        pltpu.VMEM((T // SSD_CHUNK, N_SSM_GROUPS, SSD_CHUNK, LANES), F32),
    ]
    kern = functools.partial(_mixer_kernel, n_seq=S, seq_tile=Lt, n_tiles=n_tiles, pos0=pos0)
    return pl.pallas_call(
        kern,
        grid=(B // S, n_tiles),
        in_specs=in_specs,
        out_specs=out_specs,
        out_shape=out_shape,
        scratch_shapes=scratch,
        compiler_params=pltpu.CompilerParams(
            dimension_semantics=("arbitrary", "arbitrary"),
            vmem_limit_bytes=V7X_VMEM_BYTES - 4 * 1024 * 1024,
        ),
        name="mixer",
    )(x, mod, pool_hist, conv_hist, h0, *consts)


def _ffn_kernel(x_ref, mod_ref, n2w_ref, wg_ref, wu_ref, wd_ref, fnw_ref, y_ref, hb_scr,
                *, n_seq, seq_tile, final_norm):
    S, Lt = n_seq, seq_tile
    D = x_ref.shape[-1]
    for s in range(S):
        m = mod_ref[s]
        h = _rms(x_ref[s]) * n2w_ref[...] * (1.0 + m[:, 4 * D:5 * D]) + m[:, 3 * D:4 * D]
        hb_scr[s * Lt:(s + 1) * Lt, :] = h.astype(BF16)
    hb = hb_scr[...]
    act = (_silu(_dot(hb, wg_ref[...])) * _dot(hb, wu_ref[...])).astype(BF16)
    f = _dot(act, wd_ref[...])
    for s in range(S):
        x2 = x_ref[s] + mod_ref[s][:, 5 * D:6 * D] * f[s * Lt:(s + 1) * Lt, :]
        y_ref[s] = _rms(x2) * fnw_ref[...] if final_norm else x2


def _ffn(x, mod, wts, *, n_seq, seq_tile, final_norm):
    B, L, D = x.shape
    S, Lt = n_seq, seq_tile
    assert B % S == 0 and L % Lt == 0
    consts = [wts[k] for k in ("n2w", "wg", "wu", "wd", "fnw")]
    kern = functools.partial(_ffn_kernel, n_seq=S, seq_tile=Lt, final_norm=final_norm)
    return pl.pallas_call(
        kern,
        grid=(B // S, L // Lt),
        in_specs=[
            pl.BlockSpec((S, Lt, D), lambda b, t: (b, t, 0)),
            pl.BlockSpec((S, 1, mod.shape[-1]), lambda b, t: (b, 0, 0)),
        ] + [_const_spec(c) for c in consts],
        out_specs=pl.BlockSpec((S, Lt, D), lambda b, t: (b, t, 0)),
        out_shape=jax.ShapeDtypeStruct((B, L, D), F32),
        scratch_shapes=[pltpu.VMEM((S * Lt, D), BF16)],
        compiler_params=pltpu.CompilerParams(
            dimension_semantics=("arbitrary", "arbitrary"),
            vmem_limit_bytes=V7X_VMEM_BYTES - 8 * 1024 * 1024,
        ),
        name="ffn",
    )(x, mod, *consts)


def _expand_matrix(n_heads, head_dim):
    k = np.arange(LANES)[:, None]
    j = np.arange(n_heads * head_dim)[None, :]
    return jnp.asarray((k < 3 * PIECE_LANES) & ((k % PIECE_LANES) == (j // head_dim)), BF16)


def _layer_weights(l, w_in, norm1_w, norm2_w, pool_group_w, pool_scale, conv_w, conv_b, dt_bias, a_log,
                   d_skip, ssm_norm_w, w_branch_a, w_branch_b, w_out, w_ffn_gate, w_ffn_up, w_ffn_down,
                   final_norm_w):
    D = w_in.shape[1]
    d_inner = w_branch_b.shape[1]
    conv_dim = conv_w.shape[-1]
    n_heads = a_log.shape[-1]
    assert n_heads == PIECE_LANES
    head_dim = d_inner // n_heads
    c_dt = D + d_inner + conv_dim
    wi = w_in[l].astype(BF16)
    win = jnp.concatenate([wi[:, :c_dt], jnp.pad(wi[:, c_dt:c_dt + n_heads], ((0, 0), (0, LANES - n_heads))),
                           wi[:, c_dt + n_heads:]], axis=1)
    row = lambda v: v.reshape(1, -1).astype(F32)
    pad_heads = lambda v: jnp.pad(row(v), ((0, 0), (0, LANES - n_heads)))
    return dict(
        n1w=row(norm1_w[l]), n2w=row(norm2_w[l]), fnw=row(final_norm_w), win=win,
        poolw=pool_group_w[l].astype(BF16), pscale=row(pool_scale[l]),
        convw=conv_w[l].astype(F32), convb=row(conv_b[l]),
        dtb=pad_heads(dt_bias[l]), alog=pad_heads(a_log[l]),
        dskip=row(jnp.repeat(d_skip[l], head_dim)), snw=row(ssm_norm_w[l]),
        wa=w_branch_a[l].astype(BF16), wb=w_branch_b[l].astype(BF16), wout=w_out[l].astype(BF16),
        wg=w_ffn_gate[l].astype(BF16), wu=w_ffn_up[l].astype(BF16), wd=w_ffn_down[l].astype(BF16),
        e3=_expand_matrix(n_heads, head_dim),
    )


def _pad_hist(h, rows):
    return jnp.pad(h, ((0, 0), (rows - h.shape[1], 0), (0, 0)))


def kernel(x_prompt, x_sample, c_prompt, c_sample, state_pool, state_conv, state_ssm, w_ada, b_ada, norm1_w, norm2_w, w_in, pool_group_w, pool_scale, conv_w, conv_b, dt_bias, a_log, d_skip, ssm_norm_w, w_branch_a, w_branch_b, w_out, w_ffn_gate, w_ffn_up, w_ffn_down, final_norm_w):
    depth = w_in.shape[0]
    bp, lp, D = x_prompt.shape
    bs, ls, _ = x_sample.shape
    conv_dim = conv_w.shape[-1]
    _, _, n_heads, head_dim, d_state = state_ssm.shape
    gw = n_heads * head_dim // N_SSM_GROUPS
    pool_rows = state_pool.shape[2]
    conv_rows = state_conv.shape[2]

    c_all = jnp.concatenate([c_prompt, c_sample], axis=0)
    c_all = jnp.pad(c_all, ((0, (-c_all.shape[0]) % 8), (0, 0)))

    zeros_pool = jnp.zeros((bp, POOL_HIST_ROWS, D), F32)
    zeros_conv = jnp.zeros((bp, CONV_HIST_ROWS, conv_dim), F32)
    zeros_ssm = jnp.zeros((bp, N_SSM_GROUPS, gw, d_state), F32)

    xp, xs = x_prompt, x_sample
    outs = [[] for _ in range(6)]
    for l in range(depth):
        wts = _layer_weights(l, w_in, norm1_w, norm2_w, pool_group_w, pool_scale, conv_w, conv_b, dt_bias,
                             a_log, d_skip, ssm_norm_w, w_branch_a, w_branch_b, w_out, w_ffn_gate,
                             w_ffn_up, w_ffn_down, final_norm_w)
        mod = _adaln(c_all, w_ada[l], b_ada[l])
        mod_p = mod[:bp].reshape(bp, 1, -1)
        mod_s = mod[bp:bp + bs].reshape(bs, 1, -1)
        last = l == depth - 1

        xp, pool_p, conv_p, ssm_p = _mixer(xp, mod_p, zeros_pool, zeros_conv, zeros_ssm, wts,
                                           n_seq=1, seq_tile=256, pos0=0)
        xp = _ffn(xp, mod_p, wts, n_seq=1, seq_tile=512, final_norm=last)

        xs, pool_s, conv_s, ssm_s = _mixer(
            xs, mod_s, _pad_hist(state_pool[l], POOL_HIST_ROWS), _pad_hist(state_conv[l], CONV_HIST_ROWS),
            state_ssm[l].reshape(bs, N_SSM_GROUPS, gw, d_state), wts,
            n_seq=2, seq_tile=ls, pos0=PAST_LEN)
        xs = _ffn(xs, mod_s, wts, n_seq=8, seq_tile=ls, final_norm=last)

        for acc, v in zip(outs, (pool_p[:, POOL_HIST_ROWS - pool_rows:], conv_p[:, CONV_HIST_ROWS - conv_rows:],
                                 ssm_p.reshape(bp, n_heads, head_dim, d_state),
                                 pool_s[:, POOL_HIST_ROWS - pool_rows:], conv_s[:, CONV_HIST_ROWS - conv_rows:],
                                 ssm_s.reshape(bs, n_heads, head_dim, d_state))):
            acc.append(v)
    return (xp, xs) + tuple(jnp.stack(o) for o in outs)
```

```python
import functools

import numpy as np
import jax
import jax.numpy as jnp
from jax import lax
from jax.experimental import pallas as pl
from jax.experimental.pallas import tpu as pltpu

F32 = jnp.float32
BF16 = jnp.bfloat16

EPS = 1e-6
POOL_WINDOWS = (2, 4, 8, 16)
POOL_HIST_ROWS = 16
CONV_WIDTH = 4
CONV_HIST_ROWS = 8
N_SSM_GROUPS = 4
SSD_CHUNK = 64
PAST_LEN = 4096
LANES = 128
V7X_VMEM_BYTES = 64 * 1024 * 1024
PIECE_LANES = 32
COL_BLOCK = 512


def _dot(a, b):
    return jnp.dot(a, b, preferred_element_type=F32)


def _sigmoid(x):
    return 1.0 / (1.0 + jnp.exp(-x))


def _silu(x):
    return x * _sigmoid(x)


def _softplus(x):
    return jnp.maximum(x, 0.0) + jnp.log1p(jnp.exp(-jnp.abs(x)))


def _rms(x):
    return x * lax.rsqrt(jnp.mean(x * x, axis=-1, keepdims=True) + EPS)


def _pack3(v):
    hi = v.astype(BF16).astype(F32)
    r1 = v - hi
    mid = r1.astype(BF16).astype(F32)
    lo = (r1 - mid).astype(BF16).astype(F32)
    packed = hi + pltpu.roll(mid, PIECE_LANES, axis=1) + pltpu.roll(lo, 2 * PIECE_LANES, axis=1)
    return packed.astype(BF16)


def _adaln_kernel(c_ref, w_ref, b_ref, o_ref):
    c = c_ref[...]
    o_ref[...] = _dot(_silu(c).astype(BF16), w_ref[...].astype(BF16)) + b_ref[...]


def _adaln(c_all, w_ada, b_ada):
    rows, d = c_all.shape
    n = w_ada.shape[1]
    tn = 1024
    return pl.pallas_call(
        _adaln_kernel,
        grid=(n // tn,),
        in_specs=[
            pl.BlockSpec((rows, d), lambda j: (0, 0)),
            pl.BlockSpec((d, tn), lambda j: (0, j)),
            pl.BlockSpec((1, tn), lambda j: (0, j)),
        ],
        out_specs=pl.BlockSpec((rows, tn), lambda j: (0, j)),
        out_shape=jax.ShapeDtypeStruct((rows, n), F32),
        compiler_params=pltpu.CompilerParams(dimension_semantics=("arbitrary",)),
        name="adaln",
    )(c_all, w_ada, b_ada.reshape(1, n))


def _mixer_kernel(x_ref, mod_ref, poolh_ref, convh_ref, h0_ref,
                  n1w_ref, win_ref, poolw_ref, pscale_ref, convw_ref, convb_ref, dtb_ref, alog_ref,
                  dskip_ref, snw_ref, wa_ref, wb_ref, wout_ref, e3_ref,
                  x1_ref, poolo_ref, convo_ref, hout_ref,
                  pext, cext, hb_scr, aout_scr, xc_scr, acs_scr, acse_scr, y_scr, zs_scr, ht_scr, bmt_scr, cb2_scr,
                  *, n_seq, seq_tile, n_tiles, pos0):
    S, Lt = n_seq, seq_tile
    T = S * Lt
    D = x_ref.shape[-1]
    d_inner = dskip_ref.shape[-1]
    conv_dim = convb_ref.shape[-1]
    n_heads = PIECE_LANES
    head_dim = d_inner // n_heads
    gw = d_inner // N_SSM_GROUPS
    d_state = (conv_dim - d_inner) // (2 * N_SSM_GROUPS)
    pool_g = D // len(POOL_WINDOWS)
    n_slabs = conv_dim // LANES
    c_z, c_x, c_dt = D, D + d_inner, D + d_inner + conv_dim
    c_ga, c_gb = c_dt + LANES, c_dt + LANES + D
    t = pl.program_id(1)

    def seq_rows(s):
        return slice(s * Lt, (s + 1) * Lt)

    def load_history():
        for s in range(S):
            pext[s, 0:POOL_HIST_ROWS, :] = poolh_ref[s]
            for j in range(n_slabs):
                cext[s, j, 0:CONV_HIST_ROWS, :] = convh_ref[s, :, j * LANES:(j + 1) * LANES]

    def load_state():
        for s in range(S):
            for g in range(N_SSM_GROUPS):
                ht_scr[s, g] = h0_ref[s, g].T

    def store_state():
        for s in range(S):
            for g in range(N_SSM_GROUPS):
                hout_ref[s, g] = ht_scr[s, g].T

    if n_tiles == 1:
        load_history()
        load_state()
    else:
        @pl.when(t == 0)
        def _():
            load_history()
            load_state()

    for s in range(S):
        m = mod_ref[s]
        h = _rms(x_ref[s]) * n1w_ref[...] * (1.0 + m[:, D:2 * D]) + m[:, 0:D]
        hb_scr[seq_rows(s), :] = h.astype(BF16)
    hb = hb_scr[...]

    lane = lax.broadcasted_iota(jnp.int32, (1, LANES), 1)
    dt = _softplus(_dot(hb, win_ref[:, c_dt:c_dt + LANES]) + dtb_ref[...])
    dt = jnp.where(lane < n_heads, dt, 0.0)
    acs = dt * (-jnp.exp(alog_ref[...]))
    row_in_chunk = lax.broadcasted_iota(jnp.int32, (T, 1), 0) & (SSD_CHUNK - 1)
    k = 1
    while k < SSD_CHUNK:
        acs = acs + jnp.where(row_in_chunk >= k, pltpu.roll(acs, k, axis=0), 0.0)
        k *= 2
    acs_scr[...] = acs
    acs_p = _pack3(acs)
    dt_p = _pack3(dt)

    vals = {}

    def pool_in_stage():
        p = _dot(hb, win_ref[:, 0:D])
        for s in range(S):
            pext[s, POOL_HIST_ROWS:POOL_HIST_ROWS + Lt, :] = p[seq_rows(s), :]

    def pool_stage():
        pos = pos0 + t * Lt + lax.broadcasted_iota(jnp.int32, (Lt, 1), 0)
        for s in range(S):
            outs = []
            for g, w in enumerate(POOL_WINDOWS):
                e = pext[s, :, g * pool_g:(g + 1) * pool_g]
                acc = e
                k = 1
                while k < w:
                    acc = acc + pltpu.roll(acc, k, axis=0)
                    k *= 2
                cnt = jnp.minimum(pos + 1, w).astype(F32)
                mg = acc[POOL_HIST_ROWS:, :] / cnt - e[POOL_HIST_ROWS:, :]
                outs.append(_dot(mg.astype(BF16), poolw_ref[g]))
            ya = jnp.concatenate(outs, axis=1) * pscale_ref[...]
            aout_scr[seq_rows(s), :] = ya.astype(BF16)

    def gate_a_stage():
        vals["merged"] = _sigmoid(_dot(hb, win_ref[:, c_ga:c_ga + D])) * _dot(aout_scr[...], wa_ref[...])

    def gate_b_stage():
        vals["sgb"] = _sigmoid(_dot(hb, win_ref[:, c_gb:c_gb + D]))

    slabs_per_block = COL_BLOCK // LANES
    for j in range(conv_dim // COL_BLOCK):
        xb = _dot(hb, win_ref[:, c_x + j * COL_BLOCK:c_x + (j + 1) * COL_BLOCK])
        is_x = (j + 1) * COL_BLOCK <= d_inner
        if is_x:
            ecols = slice(j * COL_BLOCK, (j + 1) * COL_BLOCK)
            acse_scr[:, ecols] = _dot(acs_p, e3_ref[:, ecols])
            dtx = _dot(dt_p, e3_ref[:, ecols])
        for s in range(S):
            for q in range(slabs_per_block):
                sl = j * slabs_per_block + q
                lc = slice(sl * LANES, (sl + 1) * LANES)
                cext[s, sl, CONV_HIST_ROWS:CONV_HIST_ROWS + Lt, :] = xb[seq_rows(s), q * LANES:(q + 1) * LANES]
                w = convw_ref[:, lc]
                acc = cext[s, sl, CONV_HIST_ROWS:CONV_HIST_ROWS + Lt, :] * w[CONV_WIDTH - 1:CONV_WIDTH, :]
                acc = acc + convb_ref[:, lc]
                for k in range(1, CONV_WIDTH):
                    tap = cext[s, sl, pl.ds(CONV_HIST_ROWS - k, Lt), :]
                    acc = acc + tap * w[CONV_WIDTH - 1 - k:CONV_WIDTH - k, :]
                xc = _silu(acc)
                if is_x:
                    y_scr[seq_rows(s), lc] = dskip_ref[:, lc] * xc
                    xc = xc * dtx[seq_rows(s), q * LANES:(q + 1) * LANES]
                xc_scr[seq_rows(s), lc] = xc
                g_b = sl - d_inner // LANES
                if 0 <= g_b < N_SSM_GROUPS:
                    for c in range(Lt // SSD_CHUNK):
                        blk = xc[c * SSD_CHUNK:(c + 1) * SSD_CHUNK, :]
                        bmt_scr[s * (Lt // SSD_CHUNK) + c, g_b] = blk.T.astype(BF16)

    row64 = lax.broadcasted_iota(jnp.int32, (SSD_CHUNK, LANES), 0)
    lane64 = lax.broadcasted_iota(jnp.int32, (SSD_CHUNK, LANES), 1)
    causal2 = row64 >= (lane64 & (SSD_CHUNK - 1))
    left = lane64 < SSD_CHUNK
    left_row = lane < SSD_CHUNK
    pairs_per_group = gw // LANES
    bc0 = d_inner
    cc0 = d_inner + N_SSM_GROUPS * d_state

    for ci in range(T // SSD_CHUNK):
        rows = pl.ds(ci * SSD_CHUNK, SSD_CHUNK)
        for g in range(N_SSM_GROUPS):
            bm = xc_scr[rows, bc0 + g * d_state:bc0 + (g + 1) * d_state].astype(BF16)
            cm = xc_scr[rows, cc0 + g * d_state:cc0 + (g + 1) * d_state].astype(BF16)
            bm2 = jnp.concatenate([bm, bm], axis=0)
            cb2_scr[ci, g] = lax.dot_general(cm, bm2, (((1,), (1,)), ((), ())), preferred_element_type=F32)

    def chunk_vpu(s, c):
        r0 = s * Lt + c * SSD_CHUNK
        rows = pl.ds(r0, SSD_CHUNK)
        acs_c = acs_scr[rows, :]
        acs_t2 = jnp.concatenate([acs_c, acs_c], axis=0).T
        acse_c = acse_scr[rows, :]
        xdt = xc_scr[rows, 0:d_inner]
        m2s, bds = [], []
        for g in range(N_SSM_GROUPS):
            cb2 = cb2_scr[r0 // SSD_CHUNK, g]
            for k in range(pairs_per_group):
                l0 = g * gw + k * LANES
                hd = l0 // head_dim
                col = acse_c[:, l0:l0 + LANES]
                rowp = jnp.where(left_row, acs_t2[hd:hd + 1, :], acs_t2[hd + 1:hd + 2, :])
                decay = jnp.exp(jnp.where(causal2, col - rowp, -jnp.inf))
                m2s.append((cb2 * decay).astype(BF16))
                x2 = xdt[:, l0:l0 + LANES]
                bd = jnp.concatenate([jnp.where(left, x2, 0.0), jnp.where(left, 0.0, x2)], axis=0)
                bds.append(bd.astype(BF16))
        return m2s, bds

    def chunk_mxu(s, c, m2s, bds):
        r0 = s * Lt + c * SSD_CHUNK
        rows = pl.ds(r0, SSD_CHUNK)
        acse_c = acse_scr[rows, :]
        last = acse_scr[pl.ds(r0 + SSD_CHUNK - 1, 1), :]
        xw_b = (xc_scr[rows, 0:d_inner] * jnp.exp(last - acse_c)).astype(BF16)
        e_acs = jnp.exp(acse_c)
        e_last = jnp.exp(last)
        for g in range(N_SSM_GROUPS):
            gl = slice(g * gw, (g + 1) * gw)
            cm = xc_scr[rows, cc0 + g * d_state:cc0 + (g + 1) * d_state].astype(BF16)
            ht = ht_scr[s, g]
            yoff = _dot(cm, ht.astype(BF16)) * e_acs[:, gl]
            for k in range(pairs_per_group):
                l0 = g * gw + k * LANES
                i = g * pairs_per_group + k
                yc = _dot(m2s[i], bds[i]) + yoff[:, k * LANES:(k + 1) * LANES]
                y_scr[rows, l0:l0 + LANES] = y_scr[rows, l0:l0 + LANES] + yc
            st = _dot(bmt_scr[r0 // SSD_CHUNK, g], xw_b[:, gl])
            ht_scr[s, g] = ht * e_last[:, gl] + st

    n_zblk = d_inner // COL_BLOCK

    def z_block(i):
        zc = slice(i * COL_BLOCK, (i + 1) * COL_BLOCK)
        zs_scr[:, zc] = _silu(_dot(hb, win_ref[:, c_z + i * COL_BLOCK:c_z + (i + 1) * COL_BLOCK]))

    stages = [pool_in_stage, pool_stage, gate_a_stage, gate_b_stage]
    fillers = []
    for i in range(max(n_zblk, len(stages))):
        fillers += [functools.partial(z_block, i)] if i < n_zblk else []
        fillers += stages[i:i + 1]
    n_fill, n_chunks = len(fillers), T // SSD_CHUNK
    order = [(s, c) for c in range(Lt // SSD_CHUNK) for s in range(S)]
    ready = chunk_vpu(*order[0])
    for i, (s, c) in enumerate(order):
        nxt = chunk_vpu(*order[i + 1]) if i + 1 < len(order) else None
        chunk_mxu(s, c, *ready)
        ready = nxt
        while n_fill - len(fillers) < -(-n_fill * (i + 1) // n_chunks):
            fillers.pop(0)()

    y = y_scr[...] * zs_scr[...]
    yn = jnp.concatenate([_rms(y[:, g * gw:(g + 1) * gw]) for g in range(N_SSM_GROUPS)], axis=1)
    bout = (yn * snw_ref[...]).astype(BF16)
    merged = vals["merged"] + vals["sgb"] * _dot(bout, wb_ref[...])
    o = _dot(merged.astype(BF16), wout_ref[...])
    for s in range(S):
        x1_ref[s] = x_ref[s] + mod_ref[s][:, 2 * D:3 * D] * o[seq_rows(s), :]

    for s in range(S):
        poolo_ref[s] = pext[s, Lt:Lt + POOL_HIST_ROWS, :]
        for j in range(n_slabs):
            convo_ref[s, :, j * LANES:(j + 1) * LANES] = cext[s, j, Lt:Lt + CONV_HIST_ROWS, :]
    if n_tiles == 1:
        store_state()
    else:
        pl.when(t == n_tiles - 1)(store_state)
        for s in range(S):
            pext[s, 0:POOL_HIST_ROWS, :] = pext[s, Lt:Lt + POOL_HIST_ROWS, :]
            for j in range(n_slabs):
                cext[s, j, 0:CONV_HIST_ROWS, :] = cext[s, j, Lt:Lt + CONV_HIST_ROWS, :]


def _const_spec(arr):
    nd = arr.ndim
    return pl.BlockSpec(arr.shape, lambda b, t, _nd=nd: (0,) * _nd, pipeline_mode=pl.Buffered(1))


def _mixer(x, mod, pool_hist, conv_hist, h0, wts, *, n_seq, seq_tile, pos0):
    B, L, D = x.shape
    S, Lt = n_seq, seq_tile
    assert B % S == 0 and L % Lt == 0 and Lt % SSD_CHUNK == 0
    n_tiles = L // Lt
    T = S * Lt
    d_inner = wts["dskip"].shape[1]
    conv_dim = wts["convb"].shape[1]
    assert d_inner % COL_BLOCK == 0 and conv_dim % COL_BLOCK == 0
    assert 2 * (d_inner // PIECE_LANES) == LANES and d_inner // PIECE_LANES == SSD_CHUNK
    gw = d_inner // N_SSM_GROUPS
    d_state = h0.shape[-1]

    consts = [wts[k] for k in ("n1w", "win", "poolw", "pscale", "convw", "convb", "dtb", "alog",
                               "dskip", "snw", "wa", "wb", "wout", "e3")]
    in_specs = [
        pl.BlockSpec((S, Lt, D), lambda b, t: (b, t, 0)),
        pl.BlockSpec((S, 1, mod.shape[-1]), lambda b, t: (b, 0, 0)),
        pl.BlockSpec((S, POOL_HIST_ROWS, D), lambda b, t: (b, 0, 0)),
        pl.BlockSpec((S, CONV_HIST_ROWS, conv_dim), lambda b, t: (b, 0, 0)),
        pl.BlockSpec((S, N_SSM_GROUPS, gw, d_state), lambda b, t: (b, 0, 0, 0)),
    ] + [_const_spec(c) for c in consts]
    out_shape = [
        jax.ShapeDtypeStruct((B, L, D), F32),
        jax.ShapeDtypeStruct((B, POOL_HIST_ROWS, D), F32),
        jax.ShapeDtypeStruct((B, CONV_HIST_ROWS, conv_dim), F32),
        jax.ShapeDtypeStruct((B, N_SSM_GROUPS, gw, d_state), F32),
    ]
    out_specs = [
        pl.BlockSpec((S, Lt, D), lambda b, t: (b, t, 0)),
        pl.BlockSpec((S, POOL_HIST_ROWS, D), lambda b, t: (b, 0, 0)),
        pl.BlockSpec((S, CONV_HIST_ROWS, conv_dim), lambda b, t: (b, 0, 0)),
        pl.BlockSpec((S, N_SSM_GROUPS, gw, d_state), lambda b, t: (b, 0, 0, 0)),
    ]
    scratch = [
        pltpu.VMEM((S, POOL_HIST_ROWS + Lt, D), F32),
        pltpu.VMEM((S, conv_dim // LANES, CONV_HIST_ROWS + Lt, LANES), F32),
        pltpu.VMEM((T, D), BF16),
        pltpu.VMEM((T, D), BF16),
        pltpu.VMEM((T, conv_dim), F32),
        pltpu.VMEM((T, LANES), F32),
        pltpu.VMEM((T, d_inner), F32),
        pltpu.VMEM((T, d_inner), F32),
        pltpu.VMEM((T, d_inner), F32),
        pltpu.VMEM((S, N_SSM_GROUPS, d_state, gw), F32),
        pltpu.VMEM((T // SSD_CHUNK, N_SSM_GROUPS, d_state, SSD_CHUNK), BF16),
        pltpu.VMEM((T // SSD_CHUNK, N_SSM_GROUPS, SSD_CHUNK, LANES), F32),
    ]
    kern = functools.partial(_mixer_kernel, n_seq=S, seq_tile=Lt, n_tiles=n_tiles, pos0=pos0)
    return pl.pallas_call(
        kern,
        grid=(B // S, n_tiles),
        in_specs=in_specs,
        out_specs=out_specs,
        out_shape=out_shape,
        scratch_shapes=scratch,
        compiler_params=pltpu.CompilerParams(
            dimension_semantics=("arbitrary", "arbitrary"),
            vmem_limit_bytes=V7X_VMEM_BYTES - 4 * 1024 * 1024,
        ),
        name="mixer",
    )(x, mod, pool_hist, conv_hist, h0, *consts)


def _ffn_kernel(x_ref, mod_ref, n2w_ref, wg_ref, wu_ref, wd_ref, fnw_ref, y_ref, hb_scr,
                *, n_seq, seq_tile, final_norm):
    S, Lt = n_seq, seq_tile
    D = x_ref.shape[-1]
    for s in range(S):
        m = mod_ref[s]
        h = _rms(x_ref[s]) * n2w_ref[...] * (1.0 + m[:, 4 * D:5 * D]) + m[:, 3 * D:4 * D]
        hb_scr[s * Lt:(s + 1) * Lt, :] = h.astype(BF16)
    hb = hb_scr[...]
    act = (_silu(_dot(hb, wg_ref[...])) * _dot(hb, wu_ref[...])).astype(BF16)
    f = _dot(act, wd_ref[...])
    for s in range(S):
        x2 = x_ref[s] + mod_ref[s][:, 5 * D:6 * D] * f[s * Lt:(s + 1) * Lt, :]
        y_ref[s] = _rms(x2) * fnw_ref[...] if final_norm else x2


def _ffn(x, mod, wts, *, n_seq, seq_tile, final_norm):
    B, L, D = x.shape
    S, Lt = n_seq, seq_tile
    assert B % S == 0 and L % Lt == 0
    consts = [wts[k] for k in ("n2w", "wg", "wu", "wd", "fnw")]
    kern = functools.partial(_ffn_kernel, n_seq=S, seq_tile=Lt, final_norm=final_norm)
    return pl.pallas_call(
        kern,
        grid=(B // S, L // Lt),
        in_specs=[
            pl.BlockSpec((S, Lt, D), lambda b, t: (b, t, 0)),
            pl.BlockSpec((S, 1, mod.shape[-1]), lambda b, t: (b, 0, 0)),
        ] + [_const_spec(c) for c in consts],
        out_specs=pl.BlockSpec((S, Lt, D), lambda b, t: (b, t, 0)),
        out_shape=jax.ShapeDtypeStruct((B, L, D), F32),
        scratch_shapes=[pltpu.VMEM((S * Lt, D), BF16)],
        compiler_params=pltpu.CompilerParams(
            dimension_semantics=("arbitrary", "arbitrary"),
            vmem_limit_bytes=V7X_VMEM_BYTES - 8 * 1024 * 1024,
        ),
        name="ffn",
    )(x, mod, *consts)


def _pack_w_in_kernel(w_ref, o_ref, *, c_dt, n_heads):
    rows = w_ref.shape[0]
    n_tail = w_ref.shape[1] - c_dt - n_heads
    o_ref[:, 0:c_dt] = w_ref[:, 0:c_dt].astype(BF16)
    dt_cols = w_ref[:, c_dt:c_dt + n_heads].astype(BF16)
    o_ref[:, c_dt:c_dt + LANES] = jnp.concatenate([dt_cols, jnp.zeros((rows, LANES - n_heads), BF16)], axis=1)
    o_ref[:, c_dt + LANES:c_dt + LANES + n_tail] = w_ref[:, c_dt + n_heads:].astype(BF16)


def _pack_w_in(w, c_dt, n_heads):
    d, n_in = w.shape
    assert c_dt % LANES == 0 and (n_in - c_dt - n_heads) % LANES == 0
    n_out = n_in + LANES - n_heads
    rb = 128
    return pl.pallas_call(
        functools.partial(_pack_w_in_kernel, c_dt=c_dt, n_heads=n_heads),
        grid=(d // rb,),
        in_specs=[pl.BlockSpec((rb, n_in), lambda i: (i, 0))],
        out_specs=pl.BlockSpec((rb, n_out), lambda i: (i, 0)),
        out_shape=jax.ShapeDtypeStruct((d, n_out), BF16),
        compiler_params=pltpu.CompilerParams(dimension_semantics=("arbitrary",)),
        name="pack_w_in",
    )(w)


def _expand_matrix(n_heads, head_dim):
    k = np.arange(LANES)[:, None]
    j = np.arange(n_heads * head_dim)[None, :]
    return jnp.asarray((k < 3 * PIECE_LANES) & ((k % PIECE_LANES) == (j // head_dim)), BF16)


def _layer_weights(l, w_in, norm1_w, norm2_w, pool_group_w, pool_scale, conv_w, conv_b, dt_bias, a_log,
                   d_skip, ssm_norm_w, w_branch_a, w_branch_b, w_out, w_ffn_gate, w_ffn_up, w_ffn_down,
                   final_norm_w):
    D = w_in.shape[1]
    d_inner = w_branch_b.shape[1]
    conv_dim = conv_w.shape[-1]
    n_heads = a_log.shape[-1]
    assert n_heads == PIECE_LANES
    head_dim = d_inner // n_heads
    c_dt = D + d_inner + conv_dim
    win = _pack_w_in(w_in[l], c_dt, n_heads)
    row = lambda v: v.reshape(1, -1).astype(F32)
    pad_heads = lambda v: jnp.pad(row(v), ((0, 0), (0, LANES - n_heads)))
    return dict(
        n1w=row(norm1_w[l]), n2w=row(norm2_w[l]), fnw=row(final_norm_w), win=win,
        poolw=pool_group_w[l].astype(BF16), pscale=row(pool_scale[l]),
        convw=conv_w[l].astype(F32), convb=row(conv_b[l]),
        dtb=pad_heads(dt_bias[l]), alog=pad_heads(a_log[l]),
        dskip=row(jnp.repeat(d_skip[l], head_dim)), snw=row(ssm_norm_w[l]),
        wa=w_branch_a[l].astype(BF16), wb=w_branch_b[l].astype(BF16), wout=w_out[l].astype(BF16),
        wg=w_ffn_gate[l].astype(BF16), wu=w_ffn_up[l].astype(BF16), wd=w_ffn_down[l].astype(BF16),
        e3=_expand_matrix(n_heads, head_dim),
    )


def _pad_hist(h, rows):
    return jnp.pad(h, ((0, 0), (rows - h.shape[1], 0), (0, 0)))


def kernel(x_prompt, x_sample, c_prompt, c_sample, state_pool, state_conv, state_ssm, w_ada, b_ada, norm1_w, norm2_w, w_in, pool_group_w, pool_scale, conv_w, conv_b, dt_bias, a_log, d_skip, ssm_norm_w, w_branch_a, w_branch_b, w_out, w_ffn_gate, w_ffn_up, w_ffn_down, final_norm_w):
    depth = w_in.shape[0]
    bp, lp, D = x_prompt.shape
    bs, ls, _ = x_sample.shape
    conv_dim = conv_w.shape[-1]
    _, _, n_heads, head_dim, d_state = state_ssm.shape
    gw = n_heads * head_dim // N_SSM_GROUPS
    pool_rows = state_pool.shape[2]
    conv_rows = state_conv.shape[2]

    c_all = jnp.concatenate([c_prompt, c_sample], axis=0)
    c_all = jnp.pad(c_all, ((0, (-c_all.shape[0]) % 8), (0, 0)))

    zeros_pool = jnp.zeros((bp, POOL_HIST_ROWS, D), F32)
    zeros_conv = jnp.zeros((bp, CONV_HIST_ROWS, conv_dim), F32)
    zeros_ssm = jnp.zeros((bp, N_SSM_GROUPS, gw, d_state), F32)

    xp, xs = x_prompt, x_sample
    outs = [[] for _ in range(6)]
    for l in range(depth):
        wts = _layer_weights(l, w_in, norm1_w, norm2_w, pool_group_w, pool_scale, conv_w, conv_b, dt_bias,
                             a_log, d_skip, ssm_norm_w, w_branch_a, w_branch_b, w_out, w_ffn_gate,
                             w_ffn_up, w_ffn_down, final_norm_w)
        mod = _adaln(c_all, w_ada[l], b_ada[l])
        mod_p = mod[:bp].reshape(bp, 1, -1)
        mod_s = mod[bp:bp + bs].reshape(bs, 1, -1)
        last = l == depth - 1

        xp, pool_p, conv_p, ssm_p = _mixer(xp, mod_p, zeros_pool, zeros_conv, zeros_ssm, wts,
                                           n_seq=1, seq_tile=256, pos0=0)
        xp = _ffn(xp, mod_p, wts, n_seq=1, seq_tile=min(1024, lp), final_norm=last)

        xs, pool_s, conv_s, ssm_s = _mixer(
            xs, mod_s, _pad_hist(state_pool[l], POOL_HIST_ROWS), _pad_hist(state_conv[l], CONV_HIST_ROWS),
            state_ssm[l].reshape(bs, N_SSM_GROUPS, gw, d_state), wts,
            n_seq=2, seq_tile=ls, pos0=PAST_LEN)
        xs = _ffn(xs, mod_s, wts, n_seq=8, seq_tile=ls, final_norm=last)

        for acc, v in zip(outs, (pool_p[:, POOL_HIST_ROWS - pool_rows:], conv_p[:, CONV_HIST_ROWS - conv_rows:],
                                 ssm_p.reshape(bp, n_heads, head_dim, d_state),
                                 pool_s[:, POOL_HIST_ROWS - pool_rows:], conv_s[:, CONV_HIST_ROWS - conv_rows:],
                                 ssm_s.reshape(bs, n_heads, head_dim, d_state))):
            acc.append(v)
    return (xp, xs) + tuple(jnp.stack(o) for o in outs)
```

```python
import functools

import numpy as np
import jax
import jax.numpy as jnp
from jax import lax
from jax.experimental import pallas as pl
from jax.experimental.pallas import tpu as pltpu

F32 = jnp.float32
BF16 = jnp.bfloat16

EPS = 1e-6
POOL_WINDOWS = (2, 4, 8, 16)
POOL_HIST_ROWS = 16
CONV_WIDTH = 4
CONV_HIST_ROWS = 8
N_SSM_GROUPS = 4
SSD_CHUNK = 64
PAST_LEN = 4096
LANES = 128
V7X_VMEM_BYTES = 64 * 1024 * 1024
PIECE_LANES = 32
COL_BLOCK = 512


def _dot(a, b):
    return jnp.dot(a, b, preferred_element_type=F32)


def _sigmoid(x):
    return 1.0 / (1.0 + jnp.exp(-x))


def _silu(x):
    return x * _sigmoid(x)


def _softplus(x):
    return jnp.maximum(x, 0.0) + jnp.log1p(jnp.exp(-jnp.abs(x)))


def _rms(x):
    return x * lax.rsqrt(jnp.mean(x * x, axis=-1, keepdims=True) + EPS)


def _pack3(v):
    hi = v.astype(BF16).astype(F32)
    r1 = v - hi
    mid = r1.astype(BF16).astype(F32)
    lo = (r1 - mid).astype(BF16).astype(F32)
    packed = hi + pltpu.roll(mid, PIECE_LANES, axis=1) + pltpu.roll(lo, 2 * PIECE_LANES, axis=1)
    return packed.astype(BF16)


def _adaln_kernel(c_ref, w_ref, b_ref, o_ref):
    c = c_ref[...]
    o_ref[...] = _dot(_silu(c).astype(BF16), w_ref[...].astype(BF16)) + b_ref[...]


def _adaln(c_all, w_ada, b_ada):
    rows, d = c_all.shape
    n = w_ada.shape[1]
    tn = 1024
    return pl.pallas_call(
        _adaln_kernel,
        grid=(n // tn,),
        in_specs=[
            pl.BlockSpec((rows, d), lambda j: (0, 0)),
            pl.BlockSpec((d, tn), lambda j: (0, j)),
            pl.BlockSpec((1, tn), lambda j: (0, j)),
        ],
        out_specs=pl.BlockSpec((rows, tn), lambda j: (0, j)),
        out_shape=jax.ShapeDtypeStruct((rows, n), F32),
        compiler_params=pltpu.CompilerParams(dimension_semantics=("arbitrary",)),
        name="adaln",
    )(c_all, w_ada, b_ada.reshape(1, n))


def _mixer_kernel(x_ref, mod_ref, poolh_ref, convh_ref, h0_ref,
                  n1w_ref, win_ref, wgt_ref, poolw_ref, pscale_ref, convw_ref, convb_ref, dtb_ref, alog_ref,
                  dskip_ref, snw_ref, wa_ref, wb_ref, wout_ref, e3_ref,
                  x1_ref, poolo_ref, convo_ref, hout_ref,
                  pext, cext, hb_scr, aout_scr, xc_scr, acs_scr, acse_scr, y_scr, zs_scr, ht_scr, bmt_scr, cb2_scr,
                  *, n_seq, seq_tile, n_tiles, pos0):
    S, Lt = n_seq, seq_tile
    T = S * Lt
    D = x_ref.shape[-1]
    d_inner = dskip_ref.shape[-1]
    conv_dim = convb_ref.shape[-1]
    n_heads = PIECE_LANES
    head_dim = d_inner // n_heads
    gw = d_inner // N_SSM_GROUPS
    d_state = (conv_dim - d_inner) // (2 * N_SSM_GROUPS)
    pool_g = D // len(POOL_WINDOWS)
    n_slabs = conv_dim // LANES
    c_z, c_x = D, D + d_inner
    c_ga, c_gb = LANES, LANES + D
    t = pl.program_id(1)

    def seq_rows(s):
        return slice(s * Lt, (s + 1) * Lt)

    def load_history():
        for s in range(S):
            pext[s, 0:POOL_HIST_ROWS, :] = poolh_ref[s]
            for j in range(n_slabs):
                cext[s, j, 0:CONV_HIST_ROWS, :] = convh_ref[s, :, j * LANES:(j + 1) * LANES]

    def load_state():
        for s in range(S):
            for g in range(N_SSM_GROUPS):
                ht_scr[s, g] = h0_ref[s, g].T

    def store_state():
        for s in range(S):
            for g in range(N_SSM_GROUPS):
                hout_ref[s, g] = ht_scr[s, g].T

    if n_tiles == 1:
        load_history()
        load_state()
    else:
        @pl.when(t == 0)
        def _():
            load_history()
            load_state()

    for s in range(S):
        m = mod_ref[s]
        h = _rms(x_ref[s]) * n1w_ref[...] * (1.0 + m[:, D:2 * D]) + m[:, 0:D]
        hb_scr[seq_rows(s), :] = h.astype(BF16)
    hb = hb_scr[...]

    lane = lax.broadcasted_iota(jnp.int32, (1, LANES), 1)
    dt = _softplus(_dot(hb, wgt_ref[:, 0:LANES]) + dtb_ref[...])
    dt = jnp.where(lane < n_heads, dt, 0.0)
    acs = dt * (-jnp.exp(alog_ref[...]))
    row_in_chunk = lax.broadcasted_iota(jnp.int32, (T, 1), 0) & (SSD_CHUNK - 1)
    k = 1
    while k < SSD_CHUNK:
        acs = acs + jnp.where(row_in_chunk >= k, pltpu.roll(acs, k, axis=0), 0.0)
        k *= 2
    acs_scr[...] = acs
    acs_p = _pack3(acs)
    dt_p = _pack3(dt)

    vals = {}

    def pool_in_stage():
        p = _dot(hb, win_ref[:, 0:D])
        for s in range(S):
            pext[s, POOL_HIST_ROWS:POOL_HIST_ROWS + Lt, :] = p[seq_rows(s), :]

    def pool_stage():
        pos = pos0 + t * Lt + lax.broadcasted_iota(jnp.int32, (Lt, 1), 0)
        for s in range(S):
            outs = []
            for g, w in enumerate(POOL_WINDOWS):
                e = pext[s, :, g * pool_g:(g + 1) * pool_g]
                acc = e
                k = 1
                while k < w:
                    acc = acc + pltpu.roll(acc, k, axis=0)
                    k *= 2
                cnt = jnp.minimum(pos + 1, w).astype(F32)
                mg = acc[POOL_HIST_ROWS:, :] / cnt - e[POOL_HIST_ROWS:, :]
                outs.append(_dot(mg.astype(BF16), poolw_ref[g]))
            ya = jnp.concatenate(outs, axis=1) * pscale_ref[...]
            aout_scr[seq_rows(s), :] = ya.astype(BF16)

    def gate_a_stage():
        vals["merged"] = _sigmoid(_dot(hb, wgt_ref[:, c_ga:c_ga + D])) * _dot(aout_scr[...], wa_ref[...])

    def gate_b_stage():
        vals["sgb"] = _sigmoid(_dot(hb, wgt_ref[:, c_gb:c_gb + D]))

    slabs_per_block = COL_BLOCK // LANES
    for j in range(conv_dim // COL_BLOCK):
        xb = _dot(hb, win_ref[:, c_x + j * COL_BLOCK:c_x + (j + 1) * COL_BLOCK])
        is_x = (j + 1) * COL_BLOCK <= d_inner
        if is_x:
            ecols = slice(j * COL_BLOCK, (j + 1) * COL_BLOCK)
            acse_scr[:, ecols] = _dot(acs_p, e3_ref[:, ecols])
            dtx = _dot(dt_p, e3_ref[:, ecols])
        for s in range(S):
            for q in range(slabs_per_block):
                sl = j * slabs_per_block + q
                lc = slice(sl * LANES, (sl + 1) * LANES)
                cext[s, sl, CONV_HIST_ROWS:CONV_HIST_ROWS + Lt, :] = xb[seq_rows(s), q * LANES:(q + 1) * LANES]
                w = convw_ref[:, lc]
                acc = cext[s, sl, CONV_HIST_ROWS:CONV_HIST_ROWS + Lt, :] * w[CONV_WIDTH - 1:CONV_WIDTH, :]
                acc = acc + convb_ref[:, lc]
                for k in range(1, CONV_WIDTH):
                    tap = cext[s, sl, pl.ds(CONV_HIST_ROWS - k, Lt), :]
                    acc = acc + tap * w[CONV_WIDTH - 1 - k:CONV_WIDTH - k, :]
                xc = _silu(acc)
                if is_x:
                    y_scr[seq_rows(s), lc] = dskip_ref[:, lc] * xc
                    xc = xc * dtx[seq_rows(s), q * LANES:(q + 1) * LANES]
                xc_scr[seq_rows(s), lc] = xc
                g_b = sl - d_inner // LANES
                if 0 <= g_b < N_SSM_GROUPS:
                    for c in range(Lt // SSD_CHUNK):
                        blk = xc[c * SSD_CHUNK:(c + 1) * SSD_CHUNK, :]
                        bmt_scr[s * (Lt // SSD_CHUNK) + c, g_b] = blk.T.astype(BF16)

    row64 = lax.broadcasted_iota(jnp.int32, (SSD_CHUNK, LANES), 0)
    lane64 = lax.broadcasted_iota(jnp.int32, (SSD_CHUNK, LANES), 1)
    causal2 = row64 >= (lane64 & (SSD_CHUNK - 1))
    left = lane64 < SSD_CHUNK
    left_row = lane < SSD_CHUNK
    pairs_per_group = gw // LANES
    bc0 = d_inner
    cc0 = d_inner + N_SSM_GROUPS * d_state

    for ci in range(T // SSD_CHUNK):
        rows = pl.ds(ci * SSD_CHUNK, SSD_CHUNK)
        for g in range(N_SSM_GROUPS):
            bm = xc_scr[rows, bc0 + g * d_state:bc0 + (g + 1) * d_state].astype(BF16)
            cm = xc_scr[rows, cc0 + g * d_state:cc0 + (g + 1) * d_state].astype(BF16)
            bm2 = jnp.concatenate([bm, bm], axis=0)
            cb2_scr[ci, g] = lax.dot_general(cm, bm2, (((1,), (1,)), ((), ())), preferred_element_type=F32)

    def chunk_body(s, c):
        r0 = s * Lt + c * SSD_CHUNK
        rows = pl.ds(r0, SSD_CHUNK)
        acs_c = acs_scr[rows, :]
        acs_t2 = jnp.concatenate([acs_c, acs_c], axis=0).T
        acse_c = acse_scr[rows, :]
        last = acse_scr[pl.ds(r0 + SSD_CHUNK - 1, 1), :]
        xdt = xc_scr[rows, 0:d_inner]
        xw_b = (xdt * jnp.exp(last - acse_c)).astype(BF16)
        e_acs = jnp.exp(acse_c)
        e_last = jnp.exp(last)
        for g in range(N_SSM_GROUPS):
            gl = slice(g * gw, (g + 1) * gw)
            cm = xc_scr[rows, cc0 + g * d_state:cc0 + (g + 1) * d_state].astype(BF16)
            cb2 = cb2_scr[r0 // SSD_CHUNK, g]
            ht = ht_scr[s, g]
            yoff = _dot(cm, ht.astype(BF16)) * e_acs[:, gl]
            for k in range(pairs_per_group):
                l0 = g * gw + k * LANES
                hd = l0 // head_dim
                col = acse_c[:, l0:l0 + LANES]
                rowp = jnp.where(left_row, acs_t2[hd:hd + 1, :], acs_t2[hd + 1:hd + 2, :])
                decay = jnp.exp(jnp.where(causal2, col - rowp, -jnp.inf))
                m2 = (cb2 * decay).astype(BF16)
                x2 = xdt[:, l0:l0 + LANES]
                bd = jnp.concatenate([jnp.where(left, x2, 0.0), jnp.where(left, 0.0, x2)], axis=0)
                yc = _dot(m2, bd.astype(BF16)) + yoff[:, k * LANES:(k + 1) * LANES]
                y_scr[rows, l0:l0 + LANES] = y_scr[rows, l0:l0 + LANES] + yc
            st = _dot(bmt_scr[r0 // SSD_CHUNK, g], xw_b[:, gl])
            ht_scr[s, g] = ht * e_last[:, gl] + st

    n_zblk = d_inner // COL_BLOCK

    def z_block(i):
        zc = slice(i * COL_BLOCK, (i + 1) * COL_BLOCK)
        zs_scr[:, zc] = _silu(_dot(hb, win_ref[:, c_z + i * COL_BLOCK:c_z + (i + 1) * COL_BLOCK]))

    stages = [pool_in_stage, pool_stage, gate_a_stage, gate_b_stage]
    fillers = []
    for i in range(max(n_zblk, len(stages))):
        fillers += [functools.partial(z_block, i)] if i < n_zblk else []
        fillers += stages[i:i + 1]
    n_fill, n_chunks = len(fillers), T // SSD_CHUNK
    order = [(s, c) for c in range(Lt // SSD_CHUNK) for s in range(S)]
    for i, (s, c) in enumerate(order):
        chunk_body(s, c)
        while n_fill - len(fillers) < -(-n_fill * (i + 1) // n_chunks):
            fillers.pop(0)()

    y = y_scr[...] * zs_scr[...]
    yn = jnp.concatenate([_rms(y[:, g * gw:(g + 1) * gw]) for g in range(N_SSM_GROUPS)], axis=1)
    bout = (yn * snw_ref[...]).astype(BF16)
    merged = vals["merged"] + vals["sgb"] * _dot(bout, wb_ref[...])
    o = _dot(merged.astype(BF16), wout_ref[...])
    for s in range(S):
        x1_ref[s] = x_ref[s] + mod_ref[s][:, 2 * D:3 * D] * o[seq_rows(s), :]

    for s in range(S):
        poolo_ref[s] = pext[s, Lt:Lt + POOL_HIST_ROWS, :]
        for j in range(n_slabs):
            convo_ref[s, :, j * LANES:(j + 1) * LANES] = cext[s, j, Lt:Lt + CONV_HIST_ROWS, :]
    if n_tiles == 1:
        store_state()
    else:
        pl.when(t == n_tiles - 1)(store_state)
        for s in range(S):
            pext[s, 0:POOL_HIST_ROWS, :] = pext[s, Lt:Lt + POOL_HIST_ROWS, :]
            for j in range(n_slabs):
                cext[s, j, 0:CONV_HIST_ROWS, :] = cext[s, j, Lt:Lt + CONV_HIST_ROWS, :]


def _const_spec(arr):
    nd = arr.ndim
    return pl.BlockSpec(arr.shape, lambda b, t, _nd=nd: (0,) * _nd, pipeline_mode=pl.Buffered(1))


def _mixer(x, mod, pool_hist, conv_hist, h0, wts, *, n_seq, seq_tile, pos0):
    B, L, D = x.shape
    S, Lt = n_seq, seq_tile
    assert B % S == 0 and L % Lt == 0 and Lt % SSD_CHUNK == 0
    n_tiles = L // Lt
    T = S * Lt
    d_inner = wts["dskip"].shape[1]
    conv_dim = wts["convb"].shape[1]
    assert d_inner % COL_BLOCK == 0 and conv_dim % COL_BLOCK == 0
    assert 2 * (d_inner // PIECE_LANES) == LANES and d_inner // PIECE_LANES == SSD_CHUNK
    gw = d_inner // N_SSM_GROUPS
    d_state = h0.shape[-1]

    consts = [wts[k] for k in ("n1w", "win", "wgt", "poolw", "pscale", "convw", "convb", "dtb", "alog",
                               "dskip", "snw", "wa", "wb", "wout", "e3")]
    in_specs = [
        pl.BlockSpec((S, Lt, D), lambda b, t: (b, t, 0)),
        pl.BlockSpec((S, 1, mod.shape[-1]), lambda b, t: (b, 0, 0)),
        pl.BlockSpec((S, POOL_HIST_ROWS, D), lambda b, t: (b, 0, 0)),
        pl.BlockSpec((S, CONV_HIST_ROWS, conv_dim), lambda b, t: (b, 0, 0)),
        pl.BlockSpec((S, N_SSM_GROUPS, gw, d_state), lambda b, t: (b, 0, 0, 0)),
    ] + [_const_spec(c) for c in consts]
    out_shape = [
        jax.ShapeDtypeStruct((B, L, D), F32),
        jax.ShapeDtypeStruct((B, POOL_HIST_ROWS, D), F32),
        jax.ShapeDtypeStruct((B, CONV_HIST_ROWS, conv_dim), F32),
        jax.ShapeDtypeStruct((B, N_SSM_GROUPS, gw, d_state), F32),
    ]
    out_specs = [
        pl.BlockSpec((S, Lt, D), lambda b, t: (b, t, 0)),
        pl.BlockSpec((S, POOL_HIST_ROWS, D), lambda b, t: (b, 0, 0)),
        pl.BlockSpec((S, CONV_HIST_ROWS, conv_dim), lambda b, t: (b, 0, 0)),
        pl.BlockSpec((S, N_SSM_GROUPS, gw, d_state), lambda b, t: (b, 0, 0, 0)),
    ]
    scratch = [
        pltpu.VMEM((S, POOL_HIST_ROWS + Lt, D), F32),
        pltpu.VMEM((S, conv_dim // LANES, CONV_HIST_ROWS + Lt, LANES), F32),
        pltpu.VMEM((T, D), BF16),
        pltpu.VMEM((T, D), BF16),
        pltpu.VMEM((T, conv_dim), F32),
        pltpu.VMEM((T, LANES), F32),
        pltpu.VMEM((T, d_inner), F32),
        pltpu.VMEM((T, d_inner), F32),
        pltpu.VMEM((T, d_inner), F32),
        pltpu.VMEM((S, N_SSM_GROUPS, d_state, gw), F32),
        pltpu.VMEM((T // SSD_CHUNK, N_SSM_GROUPS, d_state, SSD_CHUNK), BF16),
        pltpu.VMEM((T // SSD_CHUNK, N_SSM_GROUPS, SSD_CHUNK, LANES), F32),
    ]
    kern = functools.partial(_mixer_kernel, n_seq=S, seq_tile=Lt, n_tiles=n_tiles, pos0=pos0)
    return pl.pallas_call(
        kern,
        grid=(B // S, n_tiles),
        in_specs=in_specs,
        out_specs=out_specs,
        out_shape=out_shape,
        scratch_shapes=scratch,
        compiler_params=pltpu.CompilerParams(
            dimension_semantics=("arbitrary", "arbitrary"),
            vmem_limit_bytes=V7X_VMEM_BYTES - 4 * 1024 * 1024,
        ),
        name="mixer",
    )(x, mod, pool_hist, conv_hist, h0, *consts)


def _ffn_kernel(x_ref, mod_ref, n2w_ref, wg_ref, wu_ref, wd_ref, fnw_ref, y_ref, hb_scr,
                *, n_seq, seq_tile, final_norm):
    S, Lt = n_seq, seq_tile
    D = x_ref.shape[-1]
    for s in range(S):
        m = mod_ref[s]
        h = _rms(x_ref[s]) * n2w_ref[...] * (1.0 + m[:, 4 * D:5 * D]) + m[:, 3 * D:4 * D]
        hb_scr[s * Lt:(s + 1) * Lt, :] = h.astype(BF16)
    hb = hb_scr[...]
    act = (_silu(_dot(hb, wg_ref[...])) * _dot(hb, wu_ref[...])).astype(BF16)
    f = _dot(act, wd_ref[...])
    for s in range(S):
        x2 = x_ref[s] + mod_ref[s][:, 5 * D:6 * D] * f[s * Lt:(s + 1) * Lt, :]
        y_ref[s] = _rms(x2) * fnw_ref[...] if final_norm else x2


def _ffn(x, mod, wts, *, n_seq, seq_tile, final_norm):
    B, L, D = x.shape
    S, Lt = n_seq, seq_tile
    assert B % S == 0 and L % Lt == 0
    consts = [wts[k] for k in ("n2w", "wg", "wu", "wd", "fnw")]
    kern = functools.partial(_ffn_kernel, n_seq=S, seq_tile=Lt, final_norm=final_norm)
    return pl.pallas_call(
        kern,
        grid=(B // S, L // Lt),
        in_specs=[
            pl.BlockSpec((S, Lt, D), lambda b, t: (b, t, 0)),
            pl.BlockSpec((S, 1, mod.shape[-1]), lambda b, t: (b, 0, 0)),
        ] + [_const_spec(c) for c in consts],
        out_specs=pl.BlockSpec((S, Lt, D), lambda b, t: (b, t, 0)),
        out_shape=jax.ShapeDtypeStruct((B, L, D), F32),
        scratch_shapes=[pltpu.VMEM((S * Lt, D), BF16)],
        compiler_params=pltpu.CompilerParams(
            dimension_semantics=("arbitrary", "arbitrary"),
            vmem_limit_bytes=V7X_VMEM_BYTES - 8 * 1024 * 1024,
        ),
        name="ffn",
    )(x, mod, *consts)


def _pack_main_kernel(wt_ref, o_ref):
    o_ref[...] = wt_ref[...].T.astype(BF16)


def _pack_gates_kernel(wt_ref, o_ref, *, n_heads):
    d = wt_ref.shape[1]
    dt_cols = wt_ref[0:n_heads, :].T.astype(BF16)
    o_ref[:, 0:LANES] = jnp.concatenate([dt_cols, jnp.zeros((d, LANES - n_heads), BF16)], axis=1)
    o_ref[:, LANES:] = wt_ref[n_heads:, :].T.astype(BF16)


def _pack_w_in(w, c_dt, n_heads):
    d, n_in = w.shape
    wt = jnp.swapaxes(w, 0, 1)
    rb = 512
    n_gate = n_in - c_dt
    assert c_dt % rb == 0 and (n_gate - n_heads) % LANES == 0 and n_heads % 8 == 0
    win = pl.pallas_call(
        _pack_main_kernel,
        grid=(c_dt // rb,),
        in_specs=[pl.BlockSpec((rb, d), lambda i: (i, 0))],
        out_specs=pl.BlockSpec((d, rb), lambda i: (0, i)),
        out_shape=jax.ShapeDtypeStruct((d, c_dt), BF16),
        compiler_params=pltpu.CompilerParams(dimension_semantics=("arbitrary",)),
        name="pack_w_in_main",
    )(wt)
    n_out = n_gate + LANES - n_heads
    wgt = pl.pallas_call(
        functools.partial(_pack_gates_kernel, n_heads=n_heads),
        grid=(1,),
        in_specs=[pl.BlockSpec((pl.Element(n_gate), pl.Element(d)), lambda i: (c_dt, 0))],
        out_specs=pl.BlockSpec((d, n_out), lambda i: (0, 0)),
        out_shape=jax.ShapeDtypeStruct((d, n_out), BF16),
        compiler_params=pltpu.CompilerParams(dimension_semantics=("arbitrary",),
                                             vmem_limit_bytes=V7X_VMEM_BYTES // 2),
        name="pack_w_in_gates",
    )(wt)
    return win, wgt


def _expand_matrix(n_heads, head_dim):
    k = np.arange(LANES)[:, None]
    j = np.arange(n_heads * head_dim)[None, :]
    return jnp.asarray((k < 3 * PIECE_LANES) & ((k % PIECE_LANES) == (j // head_dim)), BF16)


def _layer_weights(l, w_in, norm1_w, norm2_w, pool_group_w, pool_scale, conv_w, conv_b, dt_bias, a_log,
                   d_skip, ssm_norm_w, w_branch_a, w_branch_b, w_out, w_ffn_gate, w_ffn_up, w_ffn_down,
                   final_norm_w):
    D = w_in.shape[1]
    d_inner = w_branch_b.shape[1]
    conv_dim = conv_w.shape[-1]
    n_heads = a_log.shape[-1]
    assert n_heads == PIECE_LANES
    head_dim = d_inner // n_heads
    c_dt = D + d_inner + conv_dim
    win, wgt = _pack_w_in(w_in[l], c_dt, n_heads)
    row = lambda v: v.reshape(1, -1).astype(F32)
    pad_heads = lambda v: jnp.pad(row(v), ((0, 0), (0, LANES - n_heads)))
    return dict(
        n1w=row(norm1_w[l]), n2w=row(norm2_w[l]), fnw=row(final_norm_w), win=win, wgt=wgt,
        poolw=pool_group_w[l].astype(BF16), pscale=row(pool_scale[l]),
        convw=conv_w[l].astype(F32), convb=row(conv_b[l]),
        dtb=pad_heads(dt_bias[l]), alog=pad_heads(a_log[l]),
        dskip=row(jnp.repeat(d_skip[l], head_dim)), snw=row(ssm_norm_w[l]),
        wa=w_branch_a[l].astype(BF16), wb=w_branch_b[l].astype(BF16), wout=w_out[l].astype(BF16),
        wg=w_ffn_gate[l].astype(BF16), wu=w_ffn_up[l].astype(BF16), wd=w_ffn_down[l].astype(BF16),
        e3=_expand_matrix(n_heads, head_dim),
    )


def _pad_hist(h, rows):
    return jnp.pad(h, ((0, 0), (rows - h.shape[1], 0), (0, 0)))


def kernel(x_prompt, x_sample, c_prompt, c_sample, state_pool, state_conv, state_ssm, w_ada, b_ada, norm1_w, norm2_w, w_in, pool_group_w, pool_scale, conv_w, conv_b, dt_bias, a_log, d_skip, ssm_norm_w, w_branch_a, w_branch_b, w_out, w_ffn_gate, w_ffn_up, w_ffn_down, final_norm_w):
    depth = w_in.shape[0]
    bp, lp, D = x_prompt.shape
    bs, ls, _ = x_sample.shape
    conv_dim = conv_w.shape[-1]
    _, _, n_heads, head_dim, d_state = state_ssm.shape
    gw = n_heads * head_dim // N_SSM_GROUPS
    pool_rows = state_pool.shape[2]
    conv_rows = state_conv.shape[2]

    c_all = jnp.concatenate([c_prompt, c_sample], axis=0)
    c_all = jnp.pad(c_all, ((0, (-c_all.shape[0]) % 8), (0, 0)))

    zeros_pool = jnp.zeros((bp, POOL_HIST_ROWS, D), F32)
    zeros_conv = jnp.zeros((bp, CONV_HIST_ROWS, conv_dim), F32)
    zeros_ssm = jnp.zeros((bp, N_SSM_GROUPS, gw, d_state), F32)

    xp, xs = x_prompt, x_sample
    outs = [[] for _ in range(6)]
    for l in range(depth):
        wts = _layer_weights(l, w_in, norm1_w, norm2_w, pool_group_w, pool_scale, conv_w, conv_b, dt_bias,
                             a_log, d_skip, ssm_norm_w, w_branch_a, w_branch_b, w_out, w_ffn_gate,
                             w_ffn_up, w_ffn_down, final_norm_w)
        mod = _adaln(c_all, w_ada[l], b_ada[l])
        mod_p = mod[:bp].reshape(bp, 1, -1)
        mod_s = mod[bp:bp + bs].reshape(bs, 1, -1)
        last = l == depth - 1

        xp, pool_p, conv_p, ssm_p = _mixer(xp, mod_p, zeros_pool, zeros_conv, zeros_ssm, wts,
                                           n_seq=1, seq_tile=256, pos0=0)
        xp = _ffn(xp, mod_p, wts, n_seq=1, seq_tile=min(1024, lp), final_norm=last)

        xs, pool_s, conv_s, ssm_s = _mixer(
            xs, mod_s, _pad_hist(state_pool[l], POOL_HIST_ROWS), _pad_hist(state_conv[l], CONV_HIST_ROWS),
            state_ssm[l].reshape(bs, N_SSM_GROUPS, gw, d_state), wts,
            n_seq=2, seq_tile=ls, pos0=PAST_LEN)
        xs = _ffn(xs, mod_s, wts, n_seq=8, seq_tile=ls, final_norm=last)

        for acc, v in zip(outs, (pool_p[:, POOL_HIST_ROWS - pool_rows:], conv_p[:, CONV_HIST_ROWS - conv_rows:],
                                 ssm_p.reshape(bp, n_heads, head_dim, d_state),
                                 pool_s[:, POOL_HIST_ROWS - pool_rows:], conv_s[:, CONV_HIST_ROWS - conv_rows:],
                                 ssm_s.reshape(bs, n_heads, head_dim, d_state))):
            acc.append(v)
    return (xp, xs) + tuple(jnp.stack(o) for o in outs)
```

```python
import functools

import numpy as np
import jax
import jax.numpy as jnp
from jax import lax
from jax.experimental import pallas as pl
from jax.experimental.pallas import tpu as pltpu

F32 = jnp.float32
BF16 = jnp.bfloat16

EPS = 1e-6
POOL_WINDOWS = (2, 4, 8, 16)
POOL_HIST_ROWS = 16
CONV_WIDTH = 4
CONV_HIST_ROWS = 8
N_SSM_GROUPS = 4
SSD_CHUNK = 64
PAST_LEN = 4096
LANES = 128
V7X_VMEM_BYTES = 64 * 1024 * 1024
PIECE_LANES = 32
COL_BLOCK = 512


def _dot(a, b):
    return jnp.dot(a, b, preferred_element_type=F32)


def _sigmoid(x):
    return 1.0 / (1.0 + jnp.exp(-x))


def _silu(x):
    return x * _sigmoid(x)


def _softplus(x):
    return jnp.maximum(x, 0.0) + jnp.log1p(jnp.exp(-jnp.abs(x)))


def _rms(x):
    return x * lax.rsqrt(jnp.mean(x * x, axis=-1, keepdims=True) + EPS)


def _pack3(v):
    hi = v.astype(BF16).astype(F32)
    r1 = v - hi
    mid = r1.astype(BF16).astype(F32)
    lo = (r1 - mid).astype(BF16).astype(F32)
    packed = hi + pltpu.roll(mid, PIECE_LANES, axis=1) + pltpu.roll(lo, 2 * PIECE_LANES, axis=1)
    return packed.astype(BF16)


def _adaln_kernel(c_ref, w_ref, b_ref, o_ref):
    c = c_ref[...]
    o_ref[...] = _dot(_silu(c).astype(BF16), w_ref[...].astype(BF16)) + b_ref[...]


def _adaln(c_all, w_ada, b_ada):
    rows, d = c_all.shape
    n = w_ada.shape[1]
    tn = 1024
    return pl.pallas_call(
        _adaln_kernel,
        grid=(n // tn,),
        in_specs=[
            pl.BlockSpec((rows, d), lambda j: (0, 0)),
            pl.BlockSpec((d, tn), lambda j: (0, j)),
            pl.BlockSpec((1, tn), lambda j: (0, j)),
        ],
        out_specs=pl.BlockSpec((rows, tn), lambda j: (0, j)),
        out_shape=jax.ShapeDtypeStruct((rows, n), F32),
        compiler_params=pltpu.CompilerParams(dimension_semantics=("arbitrary",)),
        name="adaln",
    )(c_all, w_ada, b_ada.reshape(1, n))


def _mixer_kernel(x_ref, mod_ref, poolh_ref, convh_ref, h0_ref,
                  n1w_ref, win_ref, poolw_ref, pscale_ref, convw_ref, convb_ref, dtb_ref, alog_ref,
                  dskip_ref, snw_ref, wa_ref, wb_ref, wout_ref, e3_ref,
                  x1_ref, poolo_ref, convo_ref, hout_ref,
                  pext, cext, hb_scr, aout_scr, xc_scr, acs_scr, acse_scr, y_scr, zs_scr, ht_scr, bmt_scr, cb2_scr,
                  *, n_seq, seq_tile, n_tiles, pos0):
    S, Lt = n_seq, seq_tile
    T = S * Lt
    D = x_ref.shape[-1]
    d_inner = dskip_ref.shape[-1]
    conv_dim = convb_ref.shape[-1]
    n_heads = PIECE_LANES
    head_dim = d_inner // n_heads
    gw = d_inner // N_SSM_GROUPS
    d_state = (conv_dim - d_inner) // (2 * N_SSM_GROUPS)
    pool_g = D // len(POOL_WINDOWS)
    n_slabs = conv_dim // LANES
    c_z, c_x, c_dt = D, D + d_inner, D + d_inner + conv_dim
    c_ga, c_gb = c_dt + LANES, c_dt + LANES + D
    t = pl.program_id(1)

    def seq_rows(s):
        return slice(s * Lt, (s + 1) * Lt)

    def load_history():
        for s in range(S):
            pext[s, 0:POOL_HIST_ROWS, :] = poolh_ref[s]
            for j in range(n_slabs):
                cext[s, j, 0:CONV_HIST_ROWS, :] = convh_ref[s, :, j * LANES:(j + 1) * LANES]

    def load_state():
        for s in range(S):
            for g in range(N_SSM_GROUPS):
                ht_scr[s, g] = h0_ref[s, g].T

    def store_state():
        for s in range(S):
            for g in range(N_SSM_GROUPS):
                hout_ref[s, g] = ht_scr[s, g].T

    if n_tiles == 1:
        load_history()
        load_state()
    else:
        @pl.when(t == 0)
        def _():
            load_history()
            load_state()

    for s in range(S):
        m = mod_ref[s]
        h = _rms(x_ref[s]) * n1w_ref[...] * (1.0 + m[:, D:2 * D]) + m[:, 0:D]
        hb_scr[seq_rows(s), :] = h.astype(BF16)
    hb = hb_scr[...]

    lane = lax.broadcasted_iota(jnp.int32, (1, LANES), 1)
    dt = _softplus(_dot(hb, win_ref[:, c_dt:c_dt + LANES]) + dtb_ref[...])
    dt = jnp.where(lane < n_heads, dt, 0.0)
    acs = dt * (-jnp.exp(alog_ref[...]))
    row_in_chunk = lax.broadcasted_iota(jnp.int32, (T, 1), 0) & (SSD_CHUNK - 1)
    k = 1
    while k < SSD_CHUNK:
        acs = acs + jnp.where(row_in_chunk >= k, pltpu.roll(acs, k, axis=0), 0.0)
        k *= 2
    acs_scr[...] = acs
    acs_p = _pack3(acs)
    dt_p = _pack3(dt)

    vals = {}

    def pool_in_stage():
        p = _dot(hb, win_ref[:, 0:D])
        for s in range(S):
            pext[s, POOL_HIST_ROWS:POOL_HIST_ROWS + Lt, :] = p[seq_rows(s), :]

    def pool_stage():
        pos = pos0 + t * Lt + lax.broadcasted_iota(jnp.int32, (Lt, 1), 0)
        for s in range(S):
            outs = []
            for g, w in enumerate(POOL_WINDOWS):
                e = pext[s, :, g * pool_g:(g + 1) * pool_g]
                acc = e
                k = 1
                while k < w:
                    acc = acc + pltpu.roll(acc, k, axis=0)
                    k *= 2
                cnt = jnp.minimum(pos + 1, w).astype(F32)
                mg = acc[POOL_HIST_ROWS:, :] / cnt - e[POOL_HIST_ROWS:, :]
                outs.append(_dot(mg.astype(BF16), poolw_ref[g]))
            ya = jnp.concatenate(outs, axis=1) * pscale_ref[...]
            aout_scr[seq_rows(s), :] = ya.astype(BF16)

    def gate_a_stage():
        vals["merged"] = _sigmoid(_dot(hb, win_ref[:, c_ga:c_ga + D])) * _dot(aout_scr[...], wa_ref[...])

    def gate_b_stage():
        vals["sgb"] = _sigmoid(_dot(hb, win_ref[:, c_gb:c_gb + D]))

    slabs_per_block = COL_BLOCK // LANES
    for j in range(conv_dim // COL_BLOCK):
        xb = _dot(hb, win_ref[:, c_x + j * COL_BLOCK:c_x + (j + 1) * COL_BLOCK])
        is_x = (j + 1) * COL_BLOCK <= d_inner
        if is_x:
            ecols = slice(j * COL_BLOCK, (j + 1) * COL_BLOCK)
            acse_scr[:, ecols] = _dot(acs_p, e3_ref[:, ecols])
            dtx = _dot(dt_p, e3_ref[:, ecols])
        for s in range(S):
            for q in range(slabs_per_block):
                sl = j * slabs_per_block + q
                lc = slice(sl * LANES, (sl + 1) * LANES)
                cext[s, sl, CONV_HIST_ROWS:CONV_HIST_ROWS + Lt, :] = xb[seq_rows(s), q * LANES:(q + 1) * LANES]
                w = convw_ref[:, lc]
                acc = cext[s, sl, CONV_HIST_ROWS:CONV_HIST_ROWS + Lt, :] * w[CONV_WIDTH - 1:CONV_WIDTH, :]
                acc = acc + convb_ref[:, lc]
                for k in range(1, CONV_WIDTH):
                    tap = cext[s, sl, pl.ds(CONV_HIST_ROWS - k, Lt), :]
                    acc = acc + tap * w[CONV_WIDTH - 1 - k:CONV_WIDTH - k, :]
                xc = _silu(acc)
                if is_x:
                    y_scr[seq_rows(s), lc] = dskip_ref[:, lc] * xc
                    xc = xc * dtx[seq_rows(s), q * LANES:(q + 1) * LANES]
                xc_scr[seq_rows(s), lc] = xc
                g_b = sl - d_inner // LANES
                if 0 <= g_b < N_SSM_GROUPS:
                    for c in range(Lt // SSD_CHUNK):
                        blk = xc[c * SSD_CHUNK:(c + 1) * SSD_CHUNK, :]
                        bmt_scr[s * (Lt // SSD_CHUNK) + c, g_b] = blk.T.astype(BF16)

    row64 = lax.broadcasted_iota(jnp.int32, (SSD_CHUNK, LANES), 0)
    lane64 = lax.broadcasted_iota(jnp.int32, (SSD_CHUNK, LANES), 1)
    causal2 = row64 >= (lane64 & (SSD_CHUNK - 1))
    left = lane64 < SSD_CHUNK
    left_row = lane < SSD_CHUNK
    pairs_per_group = gw // LANES
    bc0 = d_inner
    cc0 = d_inner + N_SSM_GROUPS * d_state

    for ci in range(T // SSD_CHUNK):
        rows = pl.ds(ci * SSD_CHUNK, SSD_CHUNK)
        for g in range(N_SSM_GROUPS):
            bm = xc_scr[rows, bc0 + g * d_state:bc0 + (g + 1) * d_state].astype(BF16)
            cm = xc_scr[rows, cc0 + g * d_state:cc0 + (g + 1) * d_state].astype(BF16)
            bm2 = jnp.concatenate([bm, bm], axis=0)
            cb2_scr[ci, g] = lax.dot_general(cm, bm2, (((1,), (1,)), ((), ())), preferred_element_type=F32)

    def chunk_body(s, c):
        r0 = s * Lt + c * SSD_CHUNK
        rows = pl.ds(r0, SSD_CHUNK)
        acs_c = acs_scr[rows, :]
        acs_t2 = jnp.concatenate([acs_c, acs_c], axis=0).T
        acse_c = acse_scr[rows, :]
        last = acse_scr[pl.ds(r0 + SSD_CHUNK - 1, 1), :]
        xdt = xc_scr[rows, 0:d_inner]
        xw_b = (xdt * jnp.exp(last - acse_c)).astype(BF16)
        e_acs = jnp.exp(acse_c)
        e_last = jnp.exp(last)
        for g in range(N_SSM_GROUPS):
            gl = slice(g * gw, (g + 1) * gw)
            cm = xc_scr[rows, cc0 + g * d_state:cc0 + (g + 1) * d_state].astype(BF16)
            cb2 = cb2_scr[r0 // SSD_CHUNK, g]
            ht = ht_scr[s, g]
            yoff = _dot(cm, ht.astype(BF16)) * e_acs[:, gl]
            for k in range(pairs_per_group):
                l0 = g * gw + k * LANES
                hd = l0 // head_dim
                col = acse_c[:, l0:l0 + LANES]
                rowp = jnp.where(left_row, acs_t2[hd:hd + 1, :], acs_t2[hd + 1:hd + 2, :])
                decay = jnp.exp(jnp.where(causal2, col - rowp, -jnp.inf))
                m2 = (cb2 * decay).astype(BF16)
                x2 = xdt[:, l0:l0 + LANES]
                bd = jnp.concatenate([jnp.where(left, x2, 0.0), jnp.where(left, 0.0, x2)], axis=0)
                yc = _dot(m2, bd.astype(BF16)) + yoff[:, k * LANES:(k + 1) * LANES]
                y_scr[rows, l0:l0 + LANES] = y_scr[rows, l0:l0 + LANES] + yc
            st = _dot(bmt_scr[r0 // SSD_CHUNK, g], xw_b[:, gl])
            ht_scr[s, g] = ht * e_last[:, gl] + st

    n_zblk = d_inner // COL_BLOCK

    def z_block(i):
        zc = slice(i * COL_BLOCK, (i + 1) * COL_BLOCK)
        zs_scr[:, zc] = _silu(_dot(hb, win_ref[:, c_z + i * COL_BLOCK:c_z + (i + 1) * COL_BLOCK]))

    stages = [pool_in_stage, pool_stage, gate_a_stage, gate_b_stage]
    fillers = []
    for i in range(max(n_zblk, len(stages))):
        fillers += [functools.partial(z_block, i)] if i < n_zblk else []
        fillers += stages[i:i + 1]
    n_fill, n_chunks = len(fillers), T // SSD_CHUNK
    order = [(s, c) for c in range(Lt // SSD_CHUNK) for s in range(S)]
    for i, (s, c) in enumerate(order):
        chunk_body(s, c)
        while n_fill - len(fillers) < -(-n_fill * (i + 1) // n_chunks):
            fillers.pop(0)()

    y = y_scr[...] * zs_scr[...]
    yn = jnp.concatenate([_rms(y[:, g * gw:(g + 1) * gw]) for g in range(N_SSM_GROUPS)], axis=1)
    bout = (yn * snw_ref[...]).astype(BF16)
    merged = vals["merged"] + vals["sgb"] * _dot(bout, wb_ref[...])
    o = _dot(merged.astype(BF16), wout_ref[...])
    for s in range(S):
        x1_ref[s] = x_ref[s] + mod_ref[s][:, 2 * D:3 * D] * o[seq_rows(s), :]

    for s in range(S):
        poolo_ref[s] = pext[s, Lt:Lt + POOL_HIST_ROWS, :]
        for j in range(n_slabs):
            convo_ref[s, :, j * LANES:(j + 1) * LANES] = cext[s, j, Lt:Lt + CONV_HIST_ROWS, :]
    if n_tiles == 1:
        store_state()
    else:
        pl.when(t == n_tiles - 1)(store_state)
        for s in range(S):
            pext[s, 0:POOL_HIST_ROWS, :] = pext[s, Lt:Lt + POOL_HIST_ROWS, :]
            for j in range(n_slabs):
                cext[s, j, 0:CONV_HIST_ROWS, :] = cext[s, j, Lt:Lt + CONV_HIST_ROWS, :]


def _const_spec(arr):
    nd = arr.ndim
    return pl.BlockSpec(arr.shape, lambda b, t, _nd=nd: (0,) * _nd, pipeline_mode=pl.Buffered(1))


def _mixer(x, mod, pool_hist, conv_hist, h0, wts, *, n_seq, seq_tile, pos0):
    B, L, D = x.shape
    S, Lt = n_seq, seq_tile
    assert B % S == 0 and L % Lt == 0 and Lt % SSD_CHUNK == 0
    n_tiles = L // Lt
    T = S * Lt
    d_inner = wts["dskip"].shape[1]
    conv_dim = wts["convb"].shape[1]
    assert d_inner % COL_BLOCK == 0 and conv_dim % COL_BLOCK == 0
    assert 2 * (d_inner // PIECE_LANES) == LANES and d_inner // PIECE_LANES == SSD_CHUNK
    gw = d_inner // N_SSM_GROUPS
    d_state = h0.shape[-1]

    consts = [wts[k] for k in ("n1w", "win", "poolw", "pscale", "convw", "convb", "dtb", "alog",
                               "dskip", "snw", "wa", "wb", "wout", "e3")]
    in_specs = [
        pl.BlockSpec((S, Lt, D), lambda b, t: (b, t, 0)),
        pl.BlockSpec((S, 1, mod.shape[-1]), lambda b, t: (b, 0, 0)),
        pl.BlockSpec((S, POOL_HIST_ROWS, D), lambda b, t: (b, 0, 0)),
        pl.BlockSpec((S, CONV_HIST_ROWS, conv_dim), lambda b, t: (b, 0, 0)),
        pl.BlockSpec((S, N_SSM_GROUPS, gw, d_state), lambda b, t: (b, 0, 0, 0)),
    ] + [_const_spec(c) for c in consts]
    out_shape = [
        jax.ShapeDtypeStruct((B, L, D), F32),
        jax.ShapeDtypeStruct((B, POOL_HIST_ROWS, D), F32),
        jax.ShapeDtypeStruct((B, CONV_HIST_ROWS, conv_dim), F32),
        jax.ShapeDtypeStruct((B, N_SSM_GROUPS, gw, d_state), F32),
    ]
    out_specs = [
        pl.BlockSpec((S, Lt, D), lambda b, t: (b, t, 0)),
        pl.BlockSpec((S, POOL_HIST_ROWS, D), lambda b, t: (b, 0, 0)),
        pl.BlockSpec((S, CONV_HIST_ROWS, conv_dim), lambda b, t: (b, 0, 0)),
        pl.BlockSpec((S, N_SSM_GROUPS, gw, d_state), lambda b, t: (b, 0, 0, 0)),
    ]
    scratch = [
        pltpu.VMEM((S, POOL_HIST_ROWS + Lt, D), F32),
        pltpu.VMEM((S, conv_dim // LANES, CONV_HIST_ROWS + Lt, LANES), F32),
        pltpu.VMEM((T, D), BF16),
        pltpu.VMEM((T, D), BF16),
        pltpu.VMEM((T, conv_dim), F32),
        pltpu.VMEM((T, LANES), F32),
        pltpu.VMEM((T, d_inner), F32),
        pltpu.VMEM((T, d_inner), F32),
        pltpu.VMEM((T, d_inner), F32),
        pltpu.VMEM((S, N_SSM_GROUPS, d_state, gw), F32),
        pltpu.VMEM((T // SSD_CHUNK, N_SSM_GROUPS, d_state, SSD_CHUNK), BF16),
        pltpu.VMEM((T // SSD_CHUNK, N_SSM_GROUPS, SSD_CHUNK, LANES), F32),
    ]
    kern = functools.partial(_mixer_kernel, n_seq=S, seq_tile=Lt, n_tiles=n_tiles, pos0=pos0)
    return pl.pallas_call(
        kern,
        grid=(B // S, n_tiles),
        in_specs=in_specs,
        out_specs=out_specs,
        out_shape=out_shape,
        scratch_shapes=scratch,
        compiler_params=pltpu.CompilerParams(
            dimension_semantics=("arbitrary", "arbitrary"),
            vmem_limit_bytes=V7X_VMEM_BYTES - 4 * 1024 * 1024,
        ),
        name="mixer",
    )(x, mod, pool_hist, conv_hist, h0, *consts)


def _ffn_kernel(x_ref, mod_ref, n2w_ref, wg_ref, wu_ref, wd_ref, fnw_ref, y_ref, hb_scr,
                *, n_seq, seq_tile, final_norm):
    S, Lt = n_seq, seq_tile
    D = x_ref.shape[-1]
    for s in range(S):
        m = mod_ref[s]
        h = _rms(x_ref[s]) * n2w_ref[...] * (1.0 + m[:, 4 * D:5 * D]) + m[:, 3 * D:4 * D]
        hb_scr[s * Lt:(s + 1) * Lt, :] = h.astype(BF16)
    hb = hb_scr[...]
    act = (_silu(_dot(hb, wg_ref[...])) * _dot(hb, wu_ref[...])).astype(BF16)
    f = _dot(act, wd_ref[...])
    for s in range(S):
        x2 = x_ref[s] + mod_ref[s][:, 5 * D:6 * D] * f[s * Lt:(s + 1) * Lt, :]
        y_ref[s] = _rms(x2) * fnw_ref[...] if final_norm else x2


def _ffn(x, mod, wts, *, n_seq, seq_tile, final_norm):
    B, L, D = x.shape
    S, Lt = n_seq, seq_tile
    assert B % S == 0 and L % Lt == 0
    consts = [wts[k] for k in ("n2w", "wg", "wu", "wd", "fnw")]
    kern = functools.partial(_ffn_kernel, n_seq=S, seq_tile=Lt, final_norm=final_norm)
    return pl.pallas_call(
        kern,
        grid=(B // S, L // Lt),
        in_specs=[
            pl.BlockSpec((S, Lt, D), lambda b, t: (b, t, 0)),
            pl.BlockSpec((S, 1, mod.shape[-1]), lambda b, t: (b, 0, 0)),
        ] + [_const_spec(c) for c in consts],
        out_specs=pl.BlockSpec((S, Lt, D), lambda b, t: (b, t, 0)),
        out_shape=jax.ShapeDtypeStruct((B, L, D), F32),
        scratch_shapes=[pltpu.VMEM((S * Lt, D), BF16)],
        compiler_params=pltpu.CompilerParams(
            dimension_semantics=("arbitrary", "arbitrary"),
            vmem_limit_bytes=V7X_VMEM_BYTES - 8 * 1024 * 1024,
        ),
        name="ffn",
    )(x, mod, *consts)


def _pack_main_kernel(wt_ref, o_ref, *, n_blocks):
    @pl.when(pl.program_id(0) < n_blocks)
    def _():
        o_ref[...] = wt_ref[...].T.astype(BF16)

    @pl.when(pl.program_id(0) >= n_blocks)
    def _():
        o_ref[...] = jnp.zeros_like(o_ref)


def _pack_gates_kernel(wt_ref, o_ref, *, n_heads):
    d = wt_ref.shape[1]
    dt_cols = wt_ref[0:n_heads, :].T.astype(BF16)
    o_ref[:, 0:LANES] = jnp.concatenate([dt_cols, jnp.zeros((d, LANES - n_heads), BF16)], axis=1)
    o_ref[:, LANES:] = wt_ref[n_heads:, :].T.astype(BF16)


def _pack_w_in(w, c_dt, n_heads):
    d, n_in = w.shape
    wt = jnp.swapaxes(w, 0, 1)
    rb = 512
    n_gate = n_in - c_dt
    n_out = n_in + LANES - n_heads
    assert c_dt % rb == 0 and (n_gate - n_heads) % LANES == 0 and n_heads % 8 == 0
    n_blocks = c_dt // rb
    main = pl.pallas_call(
        functools.partial(_pack_main_kernel, n_blocks=n_blocks),
        grid=(pl.cdiv(n_out, rb),),
        in_specs=[pl.BlockSpec((rb, d), lambda i: (jnp.minimum(i, n_blocks - 1), 0))],
        out_specs=pl.BlockSpec((d, rb), lambda i: (0, i)),
        out_shape=jax.ShapeDtypeStruct((d, n_out), BF16),
        compiler_params=pltpu.CompilerParams(dimension_semantics=("arbitrary",)),
        name="pack_w_in_main",
    )(wt)
    gates = pl.pallas_call(
        functools.partial(_pack_gates_kernel, n_heads=n_heads),
        grid=(1,),
        in_specs=[pl.BlockSpec((pl.Element(n_gate), pl.Element(d)), lambda i: (c_dt, 0))],
        out_specs=pl.BlockSpec((d, n_out - c_dt), lambda i: (0, 0)),
        out_shape=jax.ShapeDtypeStruct((d, n_out - c_dt), BF16),
        compiler_params=pltpu.CompilerParams(dimension_semantics=("arbitrary",),
                                             vmem_limit_bytes=V7X_VMEM_BYTES // 2),
        name="pack_w_in_gates",
    )(wt)
    return lax.dynamic_update_slice(main, gates, (0, c_dt))


def _expand_matrix(n_heads, head_dim):
    k = np.arange(LANES)[:, None]
    j = np.arange(n_heads * head_dim)[None, :]
    return jnp.asarray((k < 3 * PIECE_LANES) & ((k % PIECE_LANES) == (j // head_dim)), BF16)


def _layer_weights(l, w_in, norm1_w, norm2_w, pool_group_w, pool_scale, conv_w, conv_b, dt_bias, a_log,
                   d_skip, ssm_norm_w, w_branch_a, w_branch_b, w_out, w_ffn_gate, w_ffn_up, w_ffn_down,
                   final_norm_w):
    D = w_in.shape[1]
    d_inner = w_branch_b.shape[1]
    conv_dim = conv_w.shape[-1]
    n_heads = a_log.shape[-1]
    assert n_heads == PIECE_LANES
    head_dim = d_inner // n_heads
    c_dt = D + d_inner + conv_dim
    win = _pack_w_in(w_in[l], c_dt, n_heads)
    row = lambda v: v.reshape(1, -1).astype(F32)
    pad_heads = lambda v: jnp.pad(row(v), ((0, 0), (0, LANES - n_heads)))
    return dict(
        n1w=row(norm1_w[l]), n2w=row(norm2_w[l]), fnw=row(final_norm_w), win=win,
        poolw=pool_group_w[l].astype(BF16), pscale=row(pool_scale[l]),
        convw=conv_w[l].astype(F32), convb=row(conv_b[l]),
        dtb=pad_heads(dt_bias[l]), alog=pad_heads(a_log[l]),
        dskip=row(jnp.repeat(d_skip[l], head_dim)), snw=row(ssm_norm_w[l]),
        wa=w_branch_a[l].astype(BF16), wb=w_branch_b[l].astype(BF16), wout=w_out[l].astype(BF16),
        wg=w_ffn_gate[l].astype(BF16), wu=w_ffn_up[l].astype(BF16), wd=w_ffn_down[l].astype(BF16),
        e3=_expand_matrix(n_heads, head_dim),
    )


def _pad_hist(h, rows):
    return jnp.pad(h, ((0, 0), (rows - h.shape[1], 0), (0, 0)))


def kernel(x_prompt, x_sample, c_prompt, c_sample, state_pool, state_conv, state_ssm, w_ada, b_ada, norm1_w, norm2_w, w_in, pool_group_w, pool_scale, conv_w, conv_b, dt_bias, a_log, d_skip, ssm_norm_w, w_branch_a, w_branch_b, w_out, w_ffn_gate, w_ffn_up, w_ffn_down, final_norm_w):
    depth = w_in.shape[0]
    bp, lp, D = x_prompt.shape
    bs, ls, _ = x_sample.shape
    conv_dim = conv_w.shape[-1]
    _, _, n_heads, head_dim, d_state = state_ssm.shape
    gw = n_heads * head_dim // N_SSM_GROUPS
    pool_rows = state_pool.shape[2]
    conv_rows = state_conv.shape[2]

    c_all = jnp.concatenate([c_prompt, c_sample], axis=0)
    c_all = jnp.pad(c_all, ((0, (-c_all.shape[0]) % 8), (0, 0)))

    zeros_pool = jnp.zeros((bp, POOL_HIST_ROWS, D), F32)
    zeros_conv = jnp.zeros((bp, CONV_HIST_ROWS, conv_dim), F32)
    zeros_ssm = jnp.zeros((bp, N_SSM_GROUPS, gw, d_state), F32)

    xp, xs = x_prompt, x_sample
    outs = [[] for _ in range(6)]
    for l in range(depth):
        wts = _layer_weights(l, w_in, norm1_w, norm2_w, pool_group_w, pool_scale, conv_w, conv_b, dt_bias,
                             a_log, d_skip, ssm_norm_w, w_branch_a, w_branch_b, w_out, w_ffn_gate,
                             w_ffn_up, w_ffn_down, final_norm_w)
        mod = _adaln(c_all, w_ada[l], b_ada[l])
        mod_p = mod[:bp].reshape(bp, 1, -1)
        mod_s = mod[bp:bp + bs].reshape(bs, 1, -1)
        last = l == depth - 1

        xp, pool_p, conv_p, ssm_p = _mixer(xp, mod_p, zeros_pool, zeros_conv, zeros_ssm, wts,
                                           n_seq=1, seq_tile=256, pos0=0)
        xp = _ffn(xp, mod_p, wts, n_seq=1, seq_tile=min(1024, lp), final_norm=last)

        xs, pool_s, conv_s, ssm_s = _mixer(
            xs, mod_s, _pad_hist(state_pool[l], POOL_HIST_ROWS), _pad_hist(state_conv[l], CONV_HIST_ROWS),
            state_ssm[l].reshape(bs, N_SSM_GROUPS, gw, d_state), wts,
            n_seq=2, seq_tile=ls, pos0=PAST_LEN)
        xs = _ffn(xs, mod_s, wts, n_seq=8, seq_tile=ls, final_norm=last)

        for acc, v in zip(outs, (pool_p[:, POOL_HIST_ROWS - pool_rows:], conv_p[:, CONV_HIST_ROWS - conv_rows:],
                                 ssm_p.reshape(bp, n_heads, head_dim, d_state),
                                 pool_s[:, POOL_HIST_ROWS - pool_rows:], conv_s[:, CONV_HIST_ROWS - conv_rows:],
                                 ssm_s.reshape(bs, n_heads, head_dim, d_state))):
            acc.append(v)
    return (xp, xs) + tuple(jnp.stack(o) for o in outs)
```

```python
import functools

import numpy as np
import jax
import jax.numpy as jnp
from jax import lax
from jax.experimental import pallas as pl
from jax.experimental.pallas import tpu as pltpu

F32 = jnp.float32
BF16 = jnp.bfloat16

EPS = 1e-6
POOL_WINDOWS = (2, 4, 8, 16)
POOL_STATE_ROWS = max(POOL_WINDOWS) - 1
POOL_HIST_ROWS = 16
CONV_WIDTH = 4
CONV_STATE_ROWS = CONV_WIDTH - 1
CONV_HIST_ROWS = 8
N_SSM_GROUPS = 4
SSD_CHUNK = 64
PAST_LEN = 4096
LANES = 128
V7X_VMEM_BYTES = 64 * 1024 * 1024
PIECE_LANES = 32
COL_BLOCK = 512


def _dot(a, b):
    return jnp.dot(a, b, preferred_element_type=F32)


def _sigmoid(x):
    return 1.0 / (1.0 + jnp.exp(-x))


def _silu(x):
    return x * _sigmoid(x)


def _softplus(x):
    return jnp.maximum(x, 0.0) + jnp.log1p(jnp.exp(-jnp.abs(x)))


def _rms(x):
    return x * lax.rsqrt(jnp.mean(x * x, axis=-1, keepdims=True) + EPS)


def _pack3(v):
    hi = v.astype(BF16).astype(F32)
    r1 = v - hi
    mid = r1.astype(BF16).astype(F32)
    lo = (r1 - mid).astype(BF16).astype(F32)
    packed = hi + pltpu.roll(mid, PIECE_LANES, axis=1) + pltpu.roll(lo, 2 * PIECE_LANES, axis=1)
    return packed.astype(BF16)


def _adaln_kernel(c_ref, w_ref, b_ref, o_ref):
    c = c_ref[...]
    o_ref[...] = _dot(_silu(c).astype(BF16), w_ref[...].astype(BF16)) + b_ref[...]


def _adaln(c_all, w_ada, b_ada):
    rows, d = c_all.shape
    n = w_ada.shape[1]
    tn = 1024
    return pl.pallas_call(
        _adaln_kernel,
        grid=(n // tn,),
        in_specs=[
            pl.BlockSpec((rows, d), lambda j: (0, 0)),
            pl.BlockSpec((d, tn), lambda j: (0, j)),
            pl.BlockSpec((1, tn), lambda j: (0, j)),
        ],
        out_specs=pl.BlockSpec((rows, tn), lambda j: (0, j)),
        out_shape=jax.ShapeDtypeStruct((rows, n), F32),
        compiler_params=pltpu.CompilerParams(dimension_semantics=("arbitrary",)),
        name="adaln",
    )(c_all, w_ada, b_ada.reshape(1, n))


def _mixer_kernel(x_ref, mod_ref, poolh_ref, convh_ref, h0_ref,
                  n1w_ref, win_ref, poolw_ref, pscale_ref, convw_ref, convb_ref, dtb_ref, alog_ref,
                  dskip_ref, snw_ref, wa_ref, wb_ref, wout_ref, e3_ref,
                  x1_ref, poolo_ref, convo_ref, hout_ref,
                  pext, cext, hb_scr, aout_scr, xc_scr, acs_scr, acse_scr, y_scr, zs_scr, ht_scr, bmt_scr, cb2_scr,
                  *, n_seq, seq_tile, n_tiles, pos0):
    S, Lt = n_seq, seq_tile
    T = S * Lt
    D = x_ref.shape[-1]
    d_inner = dskip_ref.shape[-1]
    conv_dim = convb_ref.shape[-1]
    n_heads = PIECE_LANES
    head_dim = d_inner // n_heads
    gw = d_inner // N_SSM_GROUPS
    d_state = (conv_dim - d_inner) // (2 * N_SSM_GROUPS)
    pool_g = D // len(POOL_WINDOWS)
    n_slabs = conv_dim // LANES
    pool_pad = POOL_HIST_ROWS - POOL_STATE_ROWS
    conv_pad = CONV_HIST_ROWS - CONV_STATE_ROWS
    c_z, c_x, c_dt = D, D + d_inner, D + d_inner + conv_dim
    c_ga, c_gb = c_dt + LANES, c_dt + LANES + D
    t = pl.program_id(1)

    def seq_rows(s):
        return slice(s * Lt, (s + 1) * Lt)

    def load_history():
        for s in range(S):
            pext[s, 0:pool_pad, :] = jnp.zeros((pool_pad, D), F32)
            pext[s, pool_pad:POOL_HIST_ROWS, :] = poolh_ref[s]
            for j in range(n_slabs):
                cext[s, j, 0:conv_pad, :] = jnp.zeros((conv_pad, LANES), F32)
                cext[s, j, conv_pad:CONV_HIST_ROWS, :] = convh_ref[s, :, j * LANES:(j + 1) * LANES]

    def load_state():
        for s in range(S):
            for g in range(N_SSM_GROUPS):
                ht_scr[s, g] = h0_ref[s, g].T

    def store_state():
        for s in range(S):
            for g in range(N_SSM_GROUPS):
                hout_ref[s, g] = ht_scr[s, g].T

    if n_tiles == 1:
        load_history()
        load_state()
    else:
        @pl.when(t == 0)
        def _():
            load_history()
            load_state()

    for s in range(S):
        m = mod_ref[s]
        h = _rms(x_ref[s]) * n1w_ref[...] * (1.0 + m[:, D:2 * D]) + m[:, 0:D]
        hb_scr[seq_rows(s), :] = h.astype(BF16)
    hb = hb_scr[...]

    lane = lax.broadcasted_iota(jnp.int32, (1, LANES), 1)
    dt = _softplus(_dot(hb, win_ref[:, c_dt:c_dt + LANES]) + dtb_ref[...])
    dt = jnp.where(lane < n_heads, dt, 0.0)
    acs = dt * (-jnp.exp(alog_ref[...]))
    row_in_chunk = lax.broadcasted_iota(jnp.int32, (T, 1), 0) & (SSD_CHUNK - 1)
    k = 1
    while k < SSD_CHUNK:
        acs = acs + jnp.where(row_in_chunk >= k, pltpu.roll(acs, k, axis=0), 0.0)
        k *= 2
    acs_scr[...] = acs
    acs_p = _pack3(acs)
    dt_p = _pack3(dt)

    vals = {}

    def pool_in_stage():
        p = _dot(hb, win_ref[:, 0:D])
        for s in range(S):
            pext[s, POOL_HIST_ROWS:POOL_HIST_ROWS + Lt, :] = p[seq_rows(s), :]

    def pool_stage():
        pos = pos0 + t * Lt + lax.broadcasted_iota(jnp.int32, (Lt, 1), 0)
        for s in range(S):
            outs = []
            for g, w in enumerate(POOL_WINDOWS):
                e = pext[s, :, g * pool_g:(g + 1) * pool_g]
                acc = e
                k = 1
                while k < w:
                    acc = acc + pltpu.roll(acc, k, axis=0)
                    k *= 2
                cnt = jnp.minimum(pos + 1, w).astype(F32)
                mg = acc[POOL_HIST_ROWS:, :] / cnt - e[POOL_HIST_ROWS:, :]
                outs.append(_dot(mg.astype(BF16), poolw_ref[g]))
            ya = jnp.concatenate(outs, axis=1) * pscale_ref[...]
            aout_scr[seq_rows(s), :] = ya.astype(BF16)

    def gate_a_stage():
        vals["merged"] = _sigmoid(_dot(hb, win_ref[:, c_ga:c_ga + D])) * _dot(aout_scr[...], wa_ref[...])

    def gate_b_stage():
        vals["sgb"] = _sigmoid(_dot(hb, win_ref[:, c_gb:c_gb + D]))

    slabs_per_block = COL_BLOCK // LANES
    for j in range(conv_dim // COL_BLOCK):
        xb = _dot(hb, win_ref[:, c_x + j * COL_BLOCK:c_x + (j + 1) * COL_BLOCK])
        is_x = (j + 1) * COL_BLOCK <= d_inner
        if is_x:
            ecols = slice(j * COL_BLOCK, (j + 1) * COL_BLOCK)
            acse_scr[:, ecols] = _dot(acs_p, e3_ref[:, ecols])
            dtx = _dot(dt_p, e3_ref[:, ecols])
        for s in range(S):
            for q in range(slabs_per_block):
                sl = j * slabs_per_block + q
                lc = slice(sl * LANES, (sl + 1) * LANES)
                cext[s, sl, CONV_HIST_ROWS:CONV_HIST_ROWS + Lt, :] = xb[seq_rows(s), q * LANES:(q + 1) * LANES]
                w = convw_ref[:, lc]
                acc = cext[s, sl, CONV_HIST_ROWS:CONV_HIST_ROWS + Lt, :] * w[CONV_WIDTH - 1:CONV_WIDTH, :]
                acc = acc + convb_ref[:, lc]
                for k in range(1, CONV_WIDTH):
                    tap = cext[s, sl, pl.ds(CONV_HIST_ROWS - k, Lt), :]
                    acc = acc + tap * w[CONV_WIDTH - 1 - k:CONV_WIDTH - k, :]
                xc = _silu(acc)
                if is_x:
                    y_scr[seq_rows(s), lc] = dskip_ref[:, lc] * xc
                    xc = xc * dtx[seq_rows(s), q * LANES:(q + 1) * LANES]
                xc_scr[seq_rows(s), lc] = xc
                g_b = sl - d_inner // LANES
                if 0 <= g_b < N_SSM_GROUPS:
                    for c in range(Lt // SSD_CHUNK):
                        blk = xc[c * SSD_CHUNK:(c + 1) * SSD_CHUNK, :]
                        bmt_scr[s * (Lt // SSD_CHUNK) + c, g_b] = blk.T.astype(BF16)

    row64 = lax.broadcasted_iota(jnp.int32, (SSD_CHUNK, LANES), 0)
    lane64 = lax.broadcasted_iota(jnp.int32, (SSD_CHUNK, LANES), 1)
    causal2 = row64 >= (lane64 & (SSD_CHUNK - 1))
    left = lane64 < SSD_CHUNK
    left_row = lane < SSD_CHUNK
    pairs_per_group = gw // LANES
    bc0 = d_inner
    cc0 = d_inner + N_SSM_GROUPS * d_state

    for ci in range(T // SSD_CHUNK):
        rows = pl.ds(ci * SSD_CHUNK, SSD_CHUNK)
        for g in range(N_SSM_GROUPS):
            bm = xc_scr[rows, bc0 + g * d_state:bc0 + (g + 1) * d_state].astype(BF16)
            cm = xc_scr[rows, cc0 + g * d_state:cc0 + (g + 1) * d_state].astype(BF16)
            bm2 = jnp.concatenate([bm, bm], axis=0)
            cb2_scr[ci, g] = lax.dot_general(cm, bm2, (((1,), (1,)), ((), ())), preferred_element_type=F32)

    def chunk_body(s, c):
        r0 = s * Lt + c * SSD_CHUNK
        rows = pl.ds(r0, SSD_CHUNK)
        acs_c = acs_scr[rows, :]
        acs_t2 = jnp.concatenate([acs_c, acs_c], axis=0).T
        acse_c = acse_scr[rows, :]
        last = acse_scr[pl.ds(r0 + SSD_CHUNK - 1, 1), :]
        xdt = xc_scr[rows, 0:d_inner]
        xw_b = (xdt * jnp.exp(last - acse_c)).astype(BF16)
        e_acs = jnp.exp(acse_c)
        e_last = jnp.exp(last)
        for g in range(N_SSM_GROUPS):
            gl = slice(g * gw, (g + 1) * gw)
            cm = xc_scr[rows, cc0 + g * d_state:cc0 + (g + 1) * d_state].astype(BF16)
            cb2 = cb2_scr[r0 // SSD_CHUNK, g]
            ht = ht_scr[s, g]
            yoff = _dot(cm, ht.astype(BF16)) * e_acs[:, gl]
            for k in range(pairs_per_group):
                l0 = g * gw + k * LANES
                hd = l0 // head_dim
                col = acse_c[:, l0:l0 + LANES]
                rowp = jnp.where(left_row, acs_t2[hd:hd + 1, :], acs_t2[hd + 1:hd + 2, :])
                decay = jnp.exp(jnp.where(causal2, col - rowp, -jnp.inf))
                m2 = (cb2 * decay).astype(BF16)
                x2 = xdt[:, l0:l0 + LANES]
                bd = jnp.concatenate([jnp.where(left, x2, 0.0), jnp.where(left, 0.0, x2)], axis=0)
                yc = _dot(m2, bd.astype(BF16)) + yoff[:, k * LANES:(k + 1) * LANES]
                y_scr[rows, l0:l0 + LANES] = y_scr[rows, l0:l0 + LANES] + yc
            st = _dot(bmt_scr[r0 // SSD_CHUNK, g], xw_b[:, gl])
            ht_scr[s, g] = ht * e_last[:, gl] + st

    n_zblk = d_inner // COL_BLOCK

    def z_block(i):
        zc = slice(i * COL_BLOCK, (i + 1) * COL_BLOCK)
        zs_scr[:, zc] = _silu(_dot(hb, win_ref[:, c_z + i * COL_BLOCK:c_z + (i + 1) * COL_BLOCK]))

    stages = [pool_in_stage, pool_stage, gate_a_stage, gate_b_stage]
    fillers = []
    for i in range(max(n_zblk, len(stages))):
        fillers += [functools.partial(z_block, i)] if i < n_zblk else []
        fillers += stages[i:i + 1]
    n_fill, n_chunks = len(fillers), T // SSD_CHUNK
    order = [(s, c) for c in range(Lt // SSD_CHUNK) for s in range(S)]
    for i, (s, c) in enumerate(order):
        chunk_body(s, c)
        while n_fill - len(fillers) < -(-n_fill * (i + 1) // n_chunks):
            fillers.pop(0)()

    y = y_scr[...] * zs_scr[...]
    yn = jnp.concatenate([_rms(y[:, g * gw:(g + 1) * gw]) for g in range(N_SSM_GROUPS)], axis=1)
    bout = (yn * snw_ref[...]).astype(BF16)
    merged = vals["merged"] + vals["sgb"] * _dot(bout, wb_ref[...])
    o = _dot(merged.astype(BF16), wout_ref[...])
    for s in range(S):
        x1_ref[s] = x_ref[s] + mod_ref[s][:, 2 * D:3 * D] * o[seq_rows(s), :]

    for s in range(S):
        poolo_ref[s] = pext[s, Lt + pool_pad:Lt + POOL_HIST_ROWS, :]
        for j in range(n_slabs):
            convo_ref[s, :, j * LANES:(j + 1) * LANES] = cext[s, j, Lt + conv_pad:Lt + CONV_HIST_ROWS, :]
    if n_tiles == 1:
        store_state()
    else:
        pl.when(t == n_tiles - 1)(store_state)
        for s in range(S):
            pext[s, 0:POOL_HIST_ROWS, :] = pext[s, Lt:Lt + POOL_HIST_ROWS, :]
            for j in range(n_slabs):
                cext[s, j, 0:CONV_HIST_ROWS, :] = cext[s, j, Lt:Lt + CONV_HIST_ROWS, :]


def _const_spec(arr):
    nd = arr.ndim
    return pl.BlockSpec(arr.shape, lambda b, t, _nd=nd: (0,) * _nd, pipeline_mode=pl.Buffered(1))


def _mixer(x, mod, pool_hist, conv_hist, h0, wts, *, n_seq, seq_tile, pos0):
    B, L, D = x.shape
    S, Lt = n_seq, seq_tile
    assert B % S == 0 and L % Lt == 0 and Lt % SSD_CHUNK == 0
    n_tiles = L // Lt
    T = S * Lt
    d_inner = wts["dskip"].shape[1]
    conv_dim = wts["convb"].shape[1]
    assert d_inner % COL_BLOCK == 0 and conv_dim % COL_BLOCK == 0
    assert 2 * (d_inner // PIECE_LANES) == LANES and d_inner // PIECE_LANES == SSD_CHUNK
    gw = d_inner // N_SSM_GROUPS
    d_state = h0.shape[-1]

    consts = [wts[k] for k in ("n1w", "win", "poolw", "pscale", "convw", "convb", "dtb", "alog",
                               "dskip", "snw", "wa", "wb", "wout", "e3")]
    in_specs = [
        pl.BlockSpec((S, Lt, D), lambda b, t: (b, t, 0)),
        pl.BlockSpec((S, 1, mod.shape[-1]), lambda b, t: (b, 0, 0)),
        pl.BlockSpec((S, POOL_STATE_ROWS, D), lambda b, t: (b, 0, 0)),
        pl.BlockSpec((S, CONV_STATE_ROWS, conv_dim), lambda b, t: (b, 0, 0)),
        pl.BlockSpec((S, N_SSM_GROUPS, gw, d_state), lambda b, t: (b, 0, 0, 0)),
    ] + [_const_spec(c) for c in consts]
    out_shape = [
        jax.ShapeDtypeStruct((B, L, D), F32),
        jax.ShapeDtypeStruct((B, POOL_STATE_ROWS, D), F32),
        jax.ShapeDtypeStruct((B, CONV_STATE_ROWS, conv_dim), F32),
        jax.ShapeDtypeStruct((B, N_SSM_GROUPS, gw, d_state), F32),
    ]
    out_specs = [
        pl.BlockSpec((S, Lt, D), lambda b, t: (b, t, 0)),
        pl.BlockSpec((S, POOL_STATE_ROWS, D), lambda b, t: (b, 0, 0)),
        pl.BlockSpec((S, CONV_STATE_ROWS, conv_dim), lambda b, t: (b, 0, 0)),
        pl.BlockSpec((S, N_SSM_GROUPS, gw, d_state), lambda b, t: (b, 0, 0, 0)),
    ]
    scratch = [
        pltpu.VMEM((S, POOL_HIST_ROWS + Lt, D), F32),
        pltpu.VMEM((S, conv_dim // LANES, CONV_HIST_ROWS + Lt, LANES), F32),
        pltpu.VMEM((T, D), BF16),
        pltpu.VMEM((T, D), BF16),
        pltpu.VMEM((T, conv_dim), F32),
        pltpu.VMEM((T, LANES), F32),
        pltpu.VMEM((T, d_inner), F32),
        pltpu.VMEM((T, d_inner), F32),
        pltpu.VMEM((T, d_inner), F32),
        pltpu.VMEM((S, N_SSM_GROUPS, d_state, gw), F32),
        pltpu.VMEM((T // SSD_CHUNK, N_SSM_GROUPS, d_state, SSD_CHUNK), BF16),
        pltpu.VMEM((T // SSD_CHUNK, N_SSM_GROUPS, SSD_CHUNK, LANES), F32),
    ]
    kern = functools.partial(_mixer_kernel, n_seq=S, seq_tile=Lt, n_tiles=n_tiles, pos0=pos0)
    return pl.pallas_call(
        kern,
        grid=(B // S, n_tiles),
        in_specs=in_specs,
        out_specs=out_specs,
        out_shape=out_shape,
        scratch_shapes=scratch,
        compiler_params=pltpu.CompilerParams(
            dimension_semantics=("arbitrary", "arbitrary"),
            vmem_limit_bytes=V7X_VMEM_BYTES - 4 * 1024 * 1024,
        ),
        name="mixer",
    )(x, mod, pool_hist, conv_hist, h0, *consts)


def _ffn_kernel(x_ref, mod_ref, n2w_ref, wg_ref, wu_ref, wd_ref, fnw_ref, y_ref, hb_scr,
                *, n_seq, seq_tile, final_norm):
    S, Lt = n_seq, seq_tile
    D = x_ref.shape[-1]
    for s in range(S):
        m = mod_ref[s]
        h = _rms(x_ref[s]) * n2w_ref[...] * (1.0 + m[:, 4 * D:5 * D]) + m[:, 3 * D:4 * D]
        hb_scr[s * Lt:(s + 1) * Lt, :] = h.astype(BF16)
    hb = hb_scr[...]
    act = (_silu(_dot(hb, wg_ref[...])) * _dot(hb, wu_ref[...])).astype(BF16)
    f = _dot(act, wd_ref[...])
    for s in range(S):
        x2 = x_ref[s] + mod_ref[s][:, 5 * D:6 * D] * f[s * Lt:(s + 1) * Lt, :]
        y_ref[s] = _rms(x2) * fnw_ref[...] if final_norm else x2


def _ffn(x, mod, wts, *, n_seq, seq_tile, final_norm):
    B, L, D = x.shape
    S, Lt = n_seq, seq_tile
    assert B % S == 0 and L % Lt == 0
    consts = [wts[k] for k in ("n2w", "wg", "wu", "wd", "fnw")]
    kern = functools.partial(_ffn_kernel, n_seq=S, seq_tile=Lt, final_norm=final_norm)
    return pl.pallas_call(
        kern,
        grid=(B // S, L // Lt),
        in_specs=[
            pl.BlockSpec((S, Lt, D), lambda b, t: (b, t, 0)),
            pl.BlockSpec((S, 1, mod.shape[-1]), lambda b, t: (b, 0, 0)),
        ] + [_const_spec(c) for c in consts],
        out_specs=pl.BlockSpec((S, Lt, D), lambda b, t: (b, t, 0)),
        out_shape=jax.ShapeDtypeStruct((B, L, D), F32),
        scratch_shapes=[pltpu.VMEM((S * Lt, D), BF16)],
        compiler_params=pltpu.CompilerParams(
            dimension_semantics=("arbitrary", "arbitrary"),
            vmem_limit_bytes=V7X_VMEM_BYTES - 8 * 1024 * 1024,
        ),
        name="ffn",
    )(x, mod, *consts)


def _pack_main_kernel(wt_ref, o_ref, *, n_blocks):
    @pl.when(pl.program_id(0) < n_blocks)
    def _():
        o_ref[...] = wt_ref[...].T.astype(BF16)

    @pl.when(pl.program_id(0) >= n_blocks)
    def _():
        o_ref[...] = jnp.zeros_like(o_ref)


def _pack_gates_kernel(wt_ref, o_ref, *, n_heads):
    d = wt_ref.shape[1]
    dt_cols = wt_ref[0:n_heads, :].T.astype(BF16)
    o_ref[:, 0:LANES] = jnp.concatenate([dt_cols, jnp.zeros((d, LANES - n_heads), BF16)], axis=1)
    o_ref[:, LANES:] = wt_ref[n_heads:, :].T.astype(BF16)


def _pack_w_in(w, c_dt, n_heads):
    d, n_in = w.shape
    wt = jnp.swapaxes(w, 0, 1)
    rb = 1024
    n_gate = n_in - c_dt
    n_out = n_in + LANES - n_heads
    assert c_dt % rb == 0 and (n_gate - n_heads) % LANES == 0 and n_heads % 8 == 0
    n_blocks = c_dt // rb
    main = pl.pallas_call(
        functools.partial(_pack_main_kernel, n_blocks=n_blocks),
        grid=(pl.cdiv(n_out, rb),),
        in_specs=[pl.BlockSpec((rb, d), lambda i: (jnp.minimum(i, n_blocks - 1), 0))],
        out_specs=pl.BlockSpec((d, rb), lambda i: (0, i)),
        out_shape=jax.ShapeDtypeStruct((d, n_out), BF16),
        compiler_params=pltpu.CompilerParams(dimension_semantics=("arbitrary",)),
        name="pack_w_in_main",
    )(wt)
    gates = pl.pallas_call(
        functools.partial(_pack_gates_kernel, n_heads=n_heads),
        grid=(1,),
        in_specs=[pl.BlockSpec((pl.Element(n_gate), pl.Element(d)), lambda i: (c_dt, 0))],
        out_specs=pl.BlockSpec((d, n_out - c_dt), lambda i: (0, 0)),
        out_shape=jax.ShapeDtypeStruct((d, n_out - c_dt), BF16),
        compiler_params=pltpu.CompilerParams(dimension_semantics=("arbitrary",),
                                             vmem_limit_bytes=V7X_VMEM_BYTES // 2),
        name="pack_w_in_gates",
    )(wt)
    return lax.dynamic_update_slice(main, gates, (0, c_dt))


def _expand_matrix(n_heads, head_dim):
    k = np.arange(LANES)[:, None]
    j = np.arange(n_heads * head_dim)[None, :]
    return jnp.asarray((k < 3 * PIECE_LANES) & ((k % PIECE_LANES) == (j // head_dim)), BF16)


def _layer_weights(l, w_in, norm1_w, norm2_w, pool_group_w, pool_scale, conv_w, conv_b, dt_bias, a_log,
                   d_skip, ssm_norm_w, w_branch_a, w_branch_b, w_out, w_ffn_gate, w_ffn_up, w_ffn_down,
                   final_norm_w):
    D = w_in.shape[1]
    d_inner = w_branch_b.shape[1]
    conv_dim = conv_w.shape[-1]
    n_heads = a_log.shape[-1]
    assert n_heads == PIECE_LANES
    head_dim = d_inner // n_heads
    c_dt = D + d_inner + conv_dim
    win = _pack_w_in(w_in[l], c_dt, n_heads)
    row = lambda v: v.reshape(1, -1).astype(F32)
    pad_heads = lambda v: jnp.pad(row(v), ((0, 0), (0, LANES - n_heads)))
    return dict(
        n1w=row(norm1_w[l]), n2w=row(norm2_w[l]), fnw=row(final_norm_w), win=win,
        poolw=pool_group_w[l].astype(BF16), pscale=row(pool_scale[l]),
        convw=conv_w[l].astype(F32), convb=row(conv_b[l]),
        dtb=pad_heads(dt_bias[l]), alog=pad_heads(a_log[l]),
        dskip=row(jnp.repeat(d_skip[l], head_dim)), snw=row(ssm_norm_w[l]),
        wa=w_branch_a[l].astype(BF16), wb=w_branch_b[l].astype(BF16), wout=w_out[l].astype(BF16),
        wg=w_ffn_gate[l].astype(BF16), wu=w_ffn_up[l].astype(BF16), wd=w_ffn_down[l].astype(BF16),
        e3=_expand_matrix(n_heads, head_dim),
    )


def kernel(x_prompt, x_sample, c_prompt, c_sample, state_pool, state_conv, state_ssm, w_ada, b_ada, norm1_w, norm2_w, w_in, pool_group_w, pool_scale, conv_w, conv_b, dt_bias, a_log, d_skip, ssm_norm_w, w_branch_a, w_branch_b, w_out, w_ffn_gate, w_ffn_up, w_ffn_down, final_norm_w):
    depth = w_in.shape[0]
    bp, lp, D = x_prompt.shape
    bs, ls, _ = x_sample.shape
    conv_dim = conv_w.shape[-1]
    _, _, n_heads, head_dim, d_state = state_ssm.shape
    gw = n_heads * head_dim // N_SSM_GROUPS
    assert state_pool.shape[2] == POOL_STATE_ROWS and state_conv.shape[2] == CONV_STATE_ROWS

    c_all = jnp.concatenate([c_prompt, c_sample], axis=0)
    c_all = jnp.pad(c_all, ((0, (-c_all.shape[0]) % 8), (0, 0)))

    zeros_pool = jnp.zeros((bp, POOL_STATE_ROWS, D), F32)
    zeros_conv = jnp.zeros((bp, CONV_STATE_ROWS, conv_dim), F32)
    zeros_ssm = jnp.zeros((bp, N_SSM_GROUPS, gw, d_state), F32)

    xp, xs = x_prompt, x_sample
    outs = [[] for _ in range(6)]
    for l in range(depth):
        wts = _layer_weights(l, w_in, norm1_w, norm2_w, pool_group_w, pool_scale, conv_w, conv_b, dt_bias,
                             a_log, d_skip, ssm_norm_w, w_branch_a, w_branch_b, w_out, w_ffn_gate,
                             w_ffn_up, w_ffn_down, final_norm_w)
        mod = _adaln(c_all, w_ada[l], b_ada[l])
        mod_p = mod[:bp].reshape(bp, 1, -1)
        mod_s = mod[bp:bp + bs].reshape(bs, 1, -1)
        last = l == depth - 1

        xp, pool_p, conv_p, ssm_p = _mixer(xp, mod_p, zeros_pool, zeros_conv, zeros_ssm, wts,
                                           n_seq=1, seq_tile=256, pos0=0)
        xp = _ffn(xp, mod_p, wts, n_seq=1, seq_tile=min(1024, lp), final_norm=last)

        xs, pool_s, conv_s, ssm_s = _mixer(
            xs, mod_s, state_pool[l], state_conv[l],
            state_ssm[l].reshape(bs, N_SSM_GROUPS, gw, d_state), wts,
            n_seq=2, seq_tile=ls, pos0=PAST_LEN)
        xs = _ffn(xs, mod_s, wts, n_seq=8, seq_tile=ls, final_norm=last)

        for acc, v in zip(outs, (pool_p, conv_p,
                                 ssm_p.reshape(bp, n_heads, head_dim, d_state),
                                 pool_s, conv_s,
                                 ssm_s.reshape(bs, n_heads, head_dim, d_state))):
            acc.append(v)
    return (xp, xs) + tuple(jnp.stack(o) for o in outs)
```
